```python
import math
import jax, jax.numpy as jnp
from jax import lax
import numpy as np

D_MODEL = 1024
BATCH = 8
SEQ = 2048
DEPTH = 2

CHUNK = 64
N_MIXERS = 2
N_A = (DEPTH + 1) // 2
N_B = DEPTH // 2
GMLP_BLOCK = 128
D_GATE = 2 * D_MODEL
A_GROUPS = 8
A_GROUP_DIM = D_GATE // A_GROUPS
CONV_WIDTH = 3
D_FF = 4 * D_MODEL
LN_EPS = 1e-5
DEEPNORM_ALPHA = (2.0 * DEPTH) ** 0.25
DEEPNORM_BETA = (8.0 * DEPTH) ** -0.25

kernel_name = "hybrid_sgu_shortconv_deepnorm_trunk"


def layer_norm(x, g, b):
    xf = x.astype(jnp.float32)
    mu = jnp.mean(xf, axis=-1, keepdims=True)
    var = jnp.mean(jnp.square(xf - mu), axis=-1, keepdims=True)
    y = (xf - mu) * lax.rsqrt(var + LN_EPS) * g.astype(jnp.float32) + b.astype(jnp.float32)
    return y.astype(x.dtype)


def spatial_gating_mixer(x, w_in, b_in, v_g, v_b, w_s, b_s, w_out, b_out):
    bsz, seq, _ = x.shape
    h = jax.nn.gelu(x @ w_in + b_in, approximate=False)
    u, v = h[..., :D_GATE], h[..., D_GATE:]
    v = layer_norm(v, v_g, v_b)
    v = v.reshape(bsz, seq // GMLP_BLOCK, GMLP_BLOCK, A_GROUPS, A_GROUP_DIM)
    chunk_id = jnp.arange(GMLP_BLOCK) // CHUNK
    mask = chunk_id[:, None] >= chunk_id[None, :]
    w = jnp.where(mask[None], w_s, jnp.zeros_like(w_s))
    v = jnp.einsum('gts,bnsgc->bntgc', w, v) + b_s.T[None, None, :, :, None]
    y = u * v.reshape(bsz, seq, D_GATE)
    return y @ w_out + b_out


def short_conv_mixer(x, w_in, conv_w, w_out):
    bch = x @ w_in
    b_gate = bch[..., :D_MODEL]
    c_gate = bch[..., D_MODEL:2 * D_MODEL]
    h = bch[..., 2 * D_MODEL:]
    h = c_gate * h
    h = lax.conv_general_dilated(
        h, conv_w[:, None, :].astype(h.dtype),
        window_strides=(1,), padding=[(CONV_WIDTH - 1, 0)],
        dimension_numbers=('NWC', 'WIO', 'NWC'),
        feature_group_count=D_MODEL)
    return (b_gate * h) @ w_out


def squared_relu_mlp(x, w1, w2):
    return jnp.square(jax.nn.relu(x @ w1)) @ w2


def setup_inputs(seed: int = 0) -> dict:
    key = jax.random.key(seed)
    ks = jax.random.split(key, 20)
    f32 = jnp.float32
    nrm = lambda k, shape, scale: jax.random.normal(k, shape, f32) * scale
    x = jax.random.normal(ks[0], (BATCH, SEQ, D_MODEL), f32)
    ln_g = 1.0 + nrm(ks[1], (DEPTH, 2, D_MODEL), 0.02)
    ln_b = nrm(ks[2], (DEPTH, 2, D_MODEL), 0.02)
    a_w_in = nrm(ks[3], (N_A, D_MODEL, 2 * D_GATE), D_MODEL ** -0.5)
    a_b_in = nrm(ks[4], (N_A, 2 * D_GATE), 0.02)
    a_v_g = 1.0 + nrm(ks[5], (N_A, D_GATE), 0.02)
    a_v_b = nrm(ks[6], (N_A, D_GATE), 0.02)
    a_w_s = nrm(ks[7], (N_A, A_GROUPS, GMLP_BLOCK, GMLP_BLOCK), GMLP_BLOCK ** -0.5)
    a_b_s = 1.0 + nrm(ks[8], (N_A, A_GROUPS, GMLP_BLOCK), 0.02)
    a_w_out = nrm(ks[9], (N_A, D_GATE, D_MODEL), DEEPNORM_BETA * D_GATE ** -0.5)
    a_b_out = nrm(ks[10], (N_A, D_MODEL), 0.02)
    b_w_in = nrm(ks[11], (N_B, D_MODEL, 3 * D_MODEL), D_MODEL ** -0.5)
    b_conv = nrm(ks[12], (N_B, CONV_WIDTH, D_MODEL), CONV_WIDTH ** -0.5)
    b_w_out = nrm(ks[13], (N_B, D_MODEL, D_MODEL), DEEPNORM_BETA * D_MODEL ** -0.5)
    mlp_w1 = nrm(ks[14], (DEPTH, D_MODEL, D_FF), D_MODEL ** -0.5)
    mlp_w2 = nrm(ks[15], (DEPTH, D_FF, D_MODEL), DEEPNORM_BETA * D_FF ** -0.5)
    return {"x": x, "ln_g": ln_g, "ln_b": ln_b,
            "a_w_in": a_w_in, "a_b_in": a_b_in, "a_v_g": a_v_g, "a_v_b": a_v_b,
            "a_w_s": a_w_s, "a_b_s": a_b_s, "a_w_out": a_w_out, "a_b_out": a_b_out,
            "b_w_in": b_w_in, "b_conv": b_conv, "b_w_out": b_w_out,
            "mlp_w1": mlp_w1, "mlp_w2": mlp_w2}


def reference(x, ln_g, ln_b, a_w_in, a_b_in, a_v_g, a_v_b, a_w_s, a_b_s, a_w_out,
              a_b_out, b_w_in, b_conv, b_w_out, mlp_w1, mlp_w2):
    alpha = jnp.asarray(DEEPNORM_ALPHA, x.dtype)
    for i in range(DEPTH):
        j = i // N_MIXERS
        if i % N_MIXERS == 0:
            mix = spatial_gating_mixer(x, a_w_in[j], a_b_in[j], a_v_g[j], a_v_b[j],
                                       a_w_s[j], a_b_s[j], a_w_out[j], a_b_out[j])
        else:
            mix = short_conv_mixer(x, b_w_in[j], b_conv[j], b_w_out[j])
        x = layer_norm(alpha * x + mix, ln_g[i, 0], ln_b[i, 0])
        x = layer_norm(alpha * x + squared_relu_mlp(x, mlp_w1[i], mlp_w2[i]),
                       ln_g[i, 1], ln_b[i, 1])
    return x
```

```python
import functools

import jax
import jax.numpy as jnp
from jax import lax
from jax.experimental import pallas as pl
from jax.experimental.pallas import tpu as pltpu

D_MODEL = 1024
D_GATE = 2 * D_MODEL
D_FF = 4 * D_MODEL
A_GROUPS = 8
A_GROUP_DIM = D_GATE // A_GROUPS
GMLP_BLOCK = 128
CHUNK = 64
CONV_WIDTH = 3
DEPTH = 2
LN_EPS = 1e-5
DEEPNORM_ALPHA = (2.0 * DEPTH) ** 0.25

SUBLANES = 8
TM = 256
N_CHUNK = 512
VMEM_LIMIT_BYTES = 56 * 1024 * 1024

_BF16 = jnp.bfloat16
_F32 = jnp.float32


def _layer_norm(z, g, b):
    mu = jnp.mean(z, axis=-1, keepdims=True)
    zc = z - mu
    var = jnp.mean(zc * zc, axis=-1, keepdims=True)
    return zc * lax.rsqrt(var + LN_EPS) * g + b


def _gelu(z):
    return 0.5 * z * (1.0 + lax.erf(z * (2.0 ** -0.5)))


def _dot(a, b):
    return jnp.dot(a, b, preferred_element_type=_F32)


def _mlp_and_norm(x1, w1_ref, w2_ref, g2, b2, hid_scr):
    x1b = x1.astype(_BF16)
    for c in range(0, D_FF, N_CHUNK):
        hid = _dot(x1b, w1_ref[:, c:c + N_CHUNK])
        hid = jnp.square(jnp.maximum(hid, 0.0))
        hid_scr[:, c:c + N_CHUNK] = hid.astype(_BF16)
    ff = _dot(hid_scr[...], w2_ref[...])
    return _layer_norm(DEEPNORM_ALPHA * x1 + ff, g2, b2)


def _layer0_kernel(x_ref, win_ref, bin_ref, vg_ref, vb_ref, ws_ref, bst_ref,
                   wout_ref, bout_ref, g1_ref, b1_ref, w1_ref, w2_ref, g2_ref,
                   b2_ref, o_ref, u_scr, v_scr, y_scr, hid_scr):
    tm = x_ref.shape[0]
    x = x_ref[...]
    xb = x.astype(_BF16)

    for c in range(0, 2 * D_GATE, N_CHUNK):
        h = _dot(xb, win_ref[:, c:c + N_CHUNK]) + bin_ref[:, c:c + N_CHUNK]
        h = _gelu(h)
        if c < D_GATE:
            u_scr[:, c:c + N_CHUNK] = h
        else:
            v_scr[:, c - D_GATE:c - D_GATE + N_CHUNK] = h

    vn = _layer_norm(v_scr[...], vg_ref[...], vb_ref[...]).astype(_BF16)

    row_chunk = lax.broadcasted_iota(jnp.int32, (GMLP_BLOCK, GMLP_BLOCK), 0) // CHUNK
    col_chunk = lax.broadcasted_iota(jnp.int32, (GMLP_BLOCK, GMLP_BLOCK), 1) // CHUNK
    mask = row_chunk >= col_chunk
    for g in range(A_GROUPS):
        cols = slice(g * A_GROUP_DIM, (g + 1) * A_GROUP_DIM)
        wg = jnp.where(mask, ws_ref[g], 0.0).astype(_BF16)
        bias = bst_ref[:, g:g + 1]
        for blk in range(tm // GMLP_BLOCK):
            rows = slice(blk * GMLP_BLOCK, (blk + 1) * GMLP_BLOCK)
            mixed = _dot(wg, vn[rows, cols]) + bias
            y_scr[rows, cols] = (u_scr[rows, cols] * mixed).astype(_BF16)

    mix = _dot(y_scr[...], wout_ref[...]) + bout_ref[...]
    x1 = _layer_norm(DEEPNORM_ALPHA * x + mix, g1_ref[...], b1_ref[...])
    o_ref[...] = _mlp_and_norm(x1, w1_ref, w2_ref, g2_ref[...], b2_ref[...], hid_scr)


def _layer1_kernel(x_ref, win_ref, conv_ref, wout_ref, g1_ref, b1_ref, w1_ref,
                   w2_ref, g2_ref, b2_ref, o_ref, carry_scr, gate_scr, hid_scr,
                   *, tiles_per_seq):
    x = x_ref[...]
    xb = x.astype(_BF16)

    @pl.when(pl.program_id(0) % tiles_per_seq == 0)
    def _():
        carry_scr[...] = jnp.zeros_like(carry_scr)

    c_gate = _dot(xb, win_ref[:, D_MODEL:2 * D_MODEL])
    hc = c_gate * _dot(xb, win_ref[:, 2 * D_MODEL:3 * D_MODEL])
    ext = jnp.concatenate([carry_scr[...], hc], axis=0)
    carry_scr[...] = hc[-SUBLANES:, :]
    conv = conv_ref[CONV_WIDTH - 1:CONV_WIDTH, :] * hc
    for k in range(CONV_WIDTH - 1):
        shift = CONV_WIDTH - 1 - k
        conv = conv + conv_ref[k:k + 1, :] * pltpu.roll(ext, shift, axis=0)[SUBLANES:, :]
    b_gate = _dot(xb, win_ref[:, 0:D_MODEL])
    gate_scr[...] = (b_gate * conv).astype(_BF16)

    mix = _dot(gate_scr[...], wout_ref[...])
    x1 = _layer_norm(DEEPNORM_ALPHA * x + mix, g1_ref[...], b1_ref[...])
    o_ref[...] = _mlp_and_norm(x1, w1_ref, w2_ref, g2_ref[...], b2_ref[...], hid_scr)


def _resident(shape):
    zeros = (0,) * len(shape)
    return pl.BlockSpec(shape, lambda i: zeros, pipeline_mode=pl.Buffered(1))


def _row(v):
    return v.reshape(1, -1)


def _run_layer(body, x2d, consts, scratch_shapes):
    m = x2d.shape[0]
    tile = pl.BlockSpec((TM, D_MODEL), lambda i: (i, 0))
    return pl.pallas_call(
        body,
        grid=(m // TM,),
        in_specs=[tile] + [_resident(c.shape) for c in consts],
        out_specs=tile,
        out_shape=jax.ShapeDtypeStruct((m, D_MODEL), _F32),
        scratch_shapes=scratch_shapes,
        compiler_params=pltpu.CompilerParams(
            dimension_semantics=("arbitrary",),
            vmem_limit_bytes=VMEM_LIMIT_BYTES),
    )(x2d, *consts)


def kernel(x, ln_g, ln_b, a_w_in, a_b_in, a_v_g, a_v_b, a_w_s, a_b_s, a_w_out,
           a_b_out, b_w_in, b_conv, b_w_out, mlp_w1, mlp_w2):
    bsz, seq, d = x.shape
    assert d == D_MODEL and seq % TM == 0 and TM % GMLP_BLOCK == 0
    x2d = x.reshape(bsz * seq, d)

    consts0 = [
        a_w_in[0].astype(_BF16), _row(a_b_in[0]), _row(a_v_g[0]), _row(a_v_b[0]),
        a_w_s[0], a_b_s[0].T, a_w_out[0].astype(_BF16), _row(a_b_out[0]),
        _row(ln_g[0, 0]), _row(ln_b[0, 0]),
        mlp_w1[0].astype(_BF16), mlp_w2[0].astype(_BF16),
        _row(ln_g[0, 1]), _row(ln_b[0, 1]),
    ]
    scratch0 = [
        pltpu.VMEM((TM, D_GATE), _F32),
        pltpu.VMEM((TM, D_GATE), _F32),
        pltpu.VMEM((TM, D_GATE), _BF16),
        pltpu.VMEM((TM, D_FF), _BF16),
    ]
    x2d = _run_layer(_layer0_kernel, x2d, consts0, scratch0)

    consts1 = [
        b_w_in[0].astype(_BF16), b_conv[0], b_w_out[0].astype(_BF16),
        _row(ln_g[1, 0]), _row(ln_b[1, 0]),
        mlp_w1[1].astype(_BF16), mlp_w2[1].astype(_BF16),
        _row(ln_g[1, 1]), _row(ln_b[1, 1]),
    ]
    scratch1 = [
        pltpu.VMEM((SUBLANES, D_MODEL), _F32),
        pltpu.VMEM((TM, D_MODEL), _BF16),
        pltpu.VMEM((TM, D_FF), _BF16),
    ]
    body1 = functools.partial(_layer1_kernel, tiles_per_seq=seq // TM)
    x2d = _run_layer(body1, x2d, consts1, scratch1)
    return x2d.reshape(bsz, seq, d)
```

```python
import functools

import jax
import jax.numpy as jnp
from jax import lax
from jax.experimental import pallas as pl
from jax.experimental.pallas import tpu as pltpu

D_MODEL = 1024
D_GATE = 2 * D_MODEL
D_FF = 4 * D_MODEL
A_GROUPS = 8
A_GROUP_DIM = D_GATE // A_GROUPS
GMLP_BLOCK = 128
CHUNK = 64
CONV_WIDTH = 3
DEPTH = 2
LN_EPS = 1e-5
DEEPNORM_ALPHA = (2.0 * DEPTH) ** 0.25

SUBLANES = 8
TM = 512
SUB = 256
N_CHUNK = 512
VMEM_LIMIT_BYTES = 60 * 1024 * 1024

_BF16 = jnp.bfloat16
_F32 = jnp.float32
_SUBS = [slice(s * SUB, (s + 1) * SUB) for s in range(TM // SUB)]


def _layer_norm(z, g, b):
    mu = jnp.mean(z, axis=-1, keepdims=True)
    zc = z - mu
    var = jnp.mean(zc * zc, axis=-1, keepdims=True)
    return zc * lax.rsqrt(var + LN_EPS) * g + b


def _gelu(z):
    return 0.5 * z * (1.0 + lax.erf(z * (2.0 ** -0.5)))


def _dot(a, b):
    return jnp.dot(a, b, preferred_element_type=_F32)


def _mlp_and_norm(x1s, w1_ref, w2_ref, g2, b2, hid_scr, o_ref):
    for r, x1 in zip(_SUBS, x1s):
        x1b = x1.astype(_BF16)
        for c in range(0, D_FF, N_CHUNK):
            hid = _dot(x1b, w1_ref[:, c:c + N_CHUNK])
            hid = jnp.square(jnp.maximum(hid, 0.0))
            hid_scr[r, c:c + N_CHUNK] = hid.astype(_BF16)
    for r, x1 in zip(_SUBS, x1s):
        ff = _dot(hid_scr[r, :], w2_ref[...])
        o_ref[r, :] = _layer_norm(DEEPNORM_ALPHA * x1 + ff, g2, b2)


def _layer0_kernel(x_ref, win_ref, bin_ref, vg_ref, vb_ref, ws_ref, bst_ref,
                   wout_ref, bout_ref, g1_ref, b1_ref, w1_ref, w2_ref, g2_ref,
                   b2_ref, o_ref, vm_scr, vn_scr, y_scr, hid_scr):
    xs = [x_ref[r, :] for r in _SUBS]
    xbs = [x.astype(_BF16) for x in xs]

    for r, xb in zip(_SUBS, xbs):
        for c in range(0, D_GATE, N_CHUNK):
            cw = slice(D_GATE + c, D_GATE + c + N_CHUNK)
            vm_scr[r, c:c + N_CHUNK] = _gelu(_dot(xb, win_ref[:, cw]) + bin_ref[:, cw])
    for r in _SUBS:
        vn_scr[r, :] = _layer_norm(vm_scr[r, :], vg_ref[...], vb_ref[...]).astype(_BF16)

    row_chunk = lax.broadcasted_iota(jnp.int32, (GMLP_BLOCK, GMLP_BLOCK), 0) // CHUNK
    col_chunk = lax.broadcasted_iota(jnp.int32, (GMLP_BLOCK, GMLP_BLOCK), 1) // CHUNK
    mask = row_chunk >= col_chunk
    for g in range(A_GROUPS):
        cols = slice(g * A_GROUP_DIM, (g + 1) * A_GROUP_DIM)
        wg = jnp.where(mask, ws_ref[g], 0.0).astype(_BF16)
        bias = bst_ref[:, g:g + 1]
        for blk in range(TM // GMLP_BLOCK):
            rows = slice(blk * GMLP_BLOCK, (blk + 1) * GMLP_BLOCK)
            vm_scr[rows, cols] = _dot(wg, vn_scr[rows, cols]) + bias

    for r, xb in zip(_SUBS, xbs):
        for c in range(0, D_GATE, N_CHUNK):
            cw = slice(c, c + N_CHUNK)
            u = _gelu(_dot(xb, win_ref[:, cw]) + bin_ref[:, cw])
            y_scr[r, cw] = (u * vm_scr[r, cw]).astype(_BF16)

    x1s = []
    for r, x in zip(_SUBS, xs):
        mix = _dot(y_scr[r, :], wout_ref[...]) + bout_ref[...]
        x1s.append(_layer_norm(DEEPNORM_ALPHA * x + mix, g1_ref[...], b1_ref[...]))
    _mlp_and_norm(x1s, w1_ref, w2_ref, g2_ref[...], b2_ref[...], hid_scr, o_ref)


def _layer1_kernel(x_ref, win_ref, conv_ref, wout_ref, g1_ref, b1_ref, w1_ref,
                   w2_ref, g2_ref, b2_ref, o_ref, carry_scr, gate_scr, hid_scr,
                   *, tiles_per_seq):
    xs = [x_ref[r, :] for r in _SUBS]
    xbs = [x.astype(_BF16) for x in xs]

    @pl.when(pl.program_id(0) % tiles_per_seq == 0)
    def _():
        carry_scr[...] = jnp.zeros_like(carry_scr)

    prev = carry_scr[...]
    for r, xb in zip(_SUBS, xbs):
        c_gate = _dot(xb, win_ref[:, D_MODEL:2 * D_MODEL])
        hc = c_gate * _dot(xb, win_ref[:, 2 * D_MODEL:3 * D_MODEL])
        ext = jnp.concatenate([prev, hc], axis=0)
        prev = hc[-SUBLANES:, :]
        conv = conv_ref[CONV_WIDTH - 1:CONV_WIDTH, :] * hc
        for k in range(CONV_WIDTH - 1):
            shift = CONV_WIDTH - 1 - k
            conv = conv + conv_ref[k:k + 1, :] * pltpu.roll(ext, shift, axis=0)[SUBLANES:, :]
        b_gate = _dot(xb, win_ref[:, 0:D_MODEL])
        gate_scr[r, :] = (b_gate * conv).astype(_BF16)
    carry_scr[...] = prev

    x1s = []
    for r, x in zip(_SUBS, xs):
        mix = _dot(gate_scr[r, :], wout_ref[...])
        x1s.append(_layer_norm(DEEPNORM_ALPHA * x + mix, g1_ref[...], b1_ref[...]))
    _mlp_and_norm(x1s, w1_ref, w2_ref, g2_ref[...], b2_ref[...], hid_scr, o_ref)


def _resident(shape):
    zeros = (0,) * len(shape)
    return pl.BlockSpec(shape, lambda i: zeros, pipeline_mode=pl.Buffered(1))


def _row(v):
    return v.reshape(1, -1)


def _run_layer(body, x2d, consts, scratch_shapes):
    m = x2d.shape[0]
    tile = pl.BlockSpec((TM, D_MODEL), lambda i: (i, 0))
    return pl.pallas_call(
        body,
        grid=(m // TM,),
        in_specs=[tile] + [_resident(c.shape) for c in consts],
        out_specs=tile,
        out_shape=jax.ShapeDtypeStruct((m, D_MODEL), _F32),
        scratch_shapes=scratch_shapes,
        compiler_params=pltpu.CompilerParams(
            dimension_semantics=("arbitrary",),
            vmem_limit_bytes=VMEM_LIMIT_BYTES),
    )(x2d, *consts)


def kernel(x, ln_g, ln_b, a_w_in, a_b_in, a_v_g, a_v_b, a_w_s, a_b_s, a_w_out,
           a_b_out, b_w_in, b_conv, b_w_out, mlp_w1, mlp_w2):
    bsz, seq, d = x.shape
    assert d == D_MODEL and seq % TM == 0 and SUB % GMLP_BLOCK == 0
    x2d = x.reshape(bsz * seq, d)

    consts0 = [
        a_w_in[0].astype(_BF16), _row(a_b_in[0]), _row(a_v_g[0]), _row(a_v_b[0]),
        a_w_s[0], a_b_s[0].T, a_w_out[0].astype(_BF16), _row(a_b_out[0]),
        _row(ln_g[0, 0]), _row(ln_b[0, 0]),
        mlp_w1[0].astype(_BF16), mlp_w2[0].astype(_BF16),
        _row(ln_g[0, 1]), _row(ln_b[0, 1]),
    ]
    scratch0 = [
        pltpu.VMEM((TM, D_GATE), _F32),
        pltpu.VMEM((TM, D_GATE), _BF16),
        pltpu.VMEM((TM, D_GATE), _BF16),
        pltpu.VMEM((TM, D_FF), _BF16),
    ]
    x2d = _run_layer(_layer0_kernel, x2d, consts0, scratch0)

    consts1 = [
        b_w_in[0].astype(_BF16), b_conv[0], b_w_out[0].astype(_BF16),
        _row(ln_g[1, 0]), _row(ln_b[1, 0]),
        mlp_w1[1].astype(_BF16), mlp_w2[1].astype(_BF16),
        _row(ln_g[1, 1]), _row(ln_b[1, 1]),
    ]
    scratch1 = [
        pltpu.VMEM((SUBLANES, D_MODEL), _F32),
        pltpu.VMEM((TM, D_MODEL), _BF16),
        pltpu.VMEM((TM, D_FF), _BF16),
    ]
    body1 = functools.partial(_layer1_kernel, tiles_per_seq=seq // TM)
    x2d = _run_layer(body1, x2d, consts1, scratch1)
    return x2d.reshape(bsz, seq, d)
```

```python
import functools

import jax
import jax.numpy as jnp
from jax import lax
from jax.experimental import pallas as pl
from jax.experimental.pallas import tpu as pltpu

D_MODEL = 1024
D_GATE = 2 * D_MODEL
D_FF = 4 * D_MODEL
A_GROUPS = 8
A_GROUP_DIM = D_GATE // A_GROUPS
GMLP_BLOCK = 128
CHUNK = 64
CONV_WIDTH = 3
DEPTH = 2
LN_EPS = 1e-5
DEEPNORM_ALPHA = (2.0 * DEPTH) ** 0.25

SUBLANES = 8
TM = 512
SUB = 256
TAIL = 128
N_CHUNK = 512
VMEM_LIMIT_BYTES = 60 * 1024 * 1024

_BF16 = jnp.bfloat16
_F32 = jnp.float32
_SUBS = [slice(s * SUB, (s + 1) * SUB) for s in range(TM // SUB)]


def _layer_norm(z, g, b):
    mu = jnp.mean(z, axis=-1, keepdims=True)
    zc = z - mu
    var = jnp.mean(zc * zc, axis=-1, keepdims=True)
    return zc * lax.rsqrt(var + LN_EPS) * g + b


def _gelu(z):
    return 0.5 * z * (1.0 + lax.erf(z * (2.0 ** -0.5)))


def _dot(a, b):
    return jnp.dot(a, b, preferred_element_type=_F32)


def _mlp_and_norm(x1s, w1_ref, w2_ref, g2, b2, hid_scr, o_ref):
    for r, x1 in zip(_SUBS, x1s):
        x1b = x1.astype(_BF16)
        for c in range(0, D_FF, N_CHUNK):
            hid = _dot(x1b, w1_ref[:, c:c + N_CHUNK])
            hid = jnp.square(jnp.maximum(hid, 0.0))
            hid_scr[r, c:c + N_CHUNK] = hid.astype(_BF16)
    for r, x1 in zip(_SUBS, x1s):
        for t in range(0, SUB, TAIL):
            rows = slice(r.start + t, r.start + t + TAIL)
            ff = _dot(hid_scr[rows, :], w2_ref[...])
            o_ref[rows, :] = _layer_norm(DEEPNORM_ALPHA * x1[t:t + TAIL, :] + ff, g2, b2)


def _cast_slabs(refs):
    n = len(refs) // 2
    for src, dst in zip(refs[:n], refs[n:]):
        dst[...] = src[...].astype(_BF16)


def _layer0_kernel(x_ref, win_ref, bin_ref, vg_ref, vb_ref, ws_ref, bst_ref,
                   wout_ref, bout_ref, g1_ref, b1_ref, w1_ref, w2_ref, g2_ref,
                   b2_ref, *rest):
    n_cast = (len(rest) - 5) // 2
    cast_src, rest = rest[:n_cast], rest[n_cast:]
    o_ref, cast_dst = rest[0], rest[1:1 + n_cast]
    vm_scr, vn_scr, y_scr, hid_scr = rest[1 + n_cast:]

    xs = [x_ref[r, :] for r in _SUBS]
    xbs = [x.astype(_BF16) for x in xs]

    row_chunk = lax.broadcasted_iota(jnp.int32, (GMLP_BLOCK, GMLP_BLOCK), 0) // CHUNK
    col_chunk = lax.broadcasted_iota(jnp.int32, (GMLP_BLOCK, GMLP_BLOCK), 1) // CHUNK
    mask = row_chunk >= col_chunk
    wgs = [jnp.where(mask, ws_ref[g], 0.0).astype(_BF16) for g in range(A_GROUPS)]

    def v_half(r, xb):
        for c in range(0, D_GATE, N_CHUNK):
            cw = slice(D_GATE + c, D_GATE + c + N_CHUNK)
            vm_scr[r, c:c + N_CHUNK] = _gelu(_dot(xb, win_ref[:, cw]) + bin_ref[:, cw])

    def v_norm(r):
        vn_scr[r, :] = _layer_norm(vm_scr[r, :], vg_ref[...], vb_ref[...]).astype(_BF16)

    def spatial(r):
        for g in range(A_GROUPS):
            cols = slice(g * A_GROUP_DIM, (g + 1) * A_GROUP_DIM)
            bias = bst_ref[:, g:g + 1]
            for t in range(0, SUB, GMLP_BLOCK):
                rows = slice(r.start + t, r.start + t + GMLP_BLOCK)
                vm_scr[rows, cols] = _dot(wgs[g], vn_scr[rows, cols]) + bias

    def u_half(r, xb):
        for c in range(0, D_GATE, N_CHUNK):
            cw = slice(c, c + N_CHUNK)
            u = _gelu(_dot(xb, win_ref[:, cw]) + bin_ref[:, cw])
            y_scr[r, cw] = (u * vm_scr[r, cw]).astype(_BF16)

    a, b = _SUBS
    v_half(a, xbs[0])
    v_norm(a)
    v_half(b, xbs[1])
    spatial(a)
    v_norm(b)
    u_half(a, xbs[0])
    spatial(b)
    u_half(b, xbs[1])
    _cast_slabs(cast_src + cast_dst)

    x1s = []
    for r, x in zip(_SUBS, xs):
        mix = _dot(y_scr[r, :], wout_ref[...]) + bout_ref[...]
        x1s.append(_layer_norm(DEEPNORM_ALPHA * x + mix, g1_ref[...], b1_ref[...]))
    _mlp_and_norm(x1s, w1_ref, w2_ref, g2_ref[...], b2_ref[...], hid_scr, o_ref)


def _layer1_kernel(x_ref, win_ref, conv_ref, wout_ref, g1_ref, b1_ref, w1_ref,
                   w2_ref, g2_ref, b2_ref, o_ref, carry_scr, gate_scr, hid_scr,
                   *, tiles_per_seq):
    xs = [x_ref[r, :] for r in _SUBS]
    xbs = [x.astype(_BF16) for x in xs]

    @pl.when(pl.program_id(0) % tiles_per_seq == 0)
    def _():
        carry_scr[...] = jnp.zeros_like(carry_scr)

    prev = carry_scr[...]
    for r, xb in zip(_SUBS, xbs):
        c_gate = _dot(xb, win_ref[:, D_MODEL:2 * D_MODEL])
        hc = c_gate * _dot(xb, win_ref[:, 2 * D_MODEL:3 * D_MODEL])
        ext = jnp.concatenate([prev, hc], axis=0)
        prev = hc[-SUBLANES:, :]
        conv = conv_ref[CONV_WIDTH - 1:CONV_WIDTH, :] * hc
        for k in range(CONV_WIDTH - 1):
            shift = CONV_WIDTH - 1 - k
            conv = conv + conv_ref[k:k + 1, :] * pltpu.roll(ext, shift, axis=0)[SUBLANES:, :]
        b_gate = _dot(xb, win_ref[:, 0:D_MODEL])
        gate_scr[r, :] = (b_gate * conv).astype(_BF16)
    carry_scr[...] = prev

    x1s = []
    for r, x in zip(_SUBS, xs):
        mix = _dot(gate_scr[r, :], wout_ref[...])
        x1s.append(_layer_norm(DEEPNORM_ALPHA * x + mix, g1_ref[...], b1_ref[...]))
    _mlp_and_norm(x1s, w1_ref, w2_ref, g2_ref[...], b2_ref[...], hid_scr, o_ref)


def _resident(shape):
    zeros = (0,) * len(shape)
    return pl.BlockSpec(shape, lambda i: zeros, pipeline_mode=pl.Buffered(1))


def _row(v):
    return v.reshape(1, -1)


def _run_layer(body, x2d, consts, scratch_shapes, cast=()):
    m = x2d.shape[0]
    steps = m // TM
    tile = pl.BlockSpec((TM, D_MODEL), lambda i: (i, 0))
    slabs = [pl.BlockSpec((w.shape[0] // steps, w.shape[1]), lambda i: (i, 0)) for w in cast]
    outs = pl.pallas_call(
        body,
        grid=(steps,),
        in_specs=[tile] + [_resident(c.shape) for c in consts] + slabs,
        out_specs=[tile] + slabs,
        out_shape=[jax.ShapeDtypeStruct((m, D_MODEL), _F32)]
        + [jax.ShapeDtypeStruct(w.shape, _BF16) for w in cast],
        scratch_shapes=scratch_shapes,
        compiler_params=pltpu.CompilerParams(
            dimension_semantics=("arbitrary",),
            vmem_limit_bytes=VMEM_LIMIT_BYTES),
    )(x2d, *consts, *cast)
    return outs[0], outs[1:]


def kernel(x, ln_g, ln_b, a_w_in, a_b_in, a_v_g, a_v_b, a_w_s, a_b_s, a_w_out,
           a_b_out, b_w_in, b_conv, b_w_out, mlp_w1, mlp_w2):
    bsz, seq, d = x.shape
    assert d == D_MODEL and seq % TM == 0 and SUB % GMLP_BLOCK == 0 and TM == 2 * SUB
    x2d = x.reshape(bsz * seq, d)

    consts0 = [
        a_w_in[0].astype(_BF16), _row(a_b_in[0]), _row(a_v_g[0]), _row(a_v_b[0]),
        a_w_s[0], a_b_s[0].T, a_w_out[0].astype(_BF16), _row(a_b_out[0]),
        _row(ln_g[0, 0]), _row(ln_b[0, 0]),
        mlp_w1[0].astype(_BF16), mlp_w2[0].astype(_BF16),
        _row(ln_g[0, 1]), _row(ln_b[0, 1]),
    ]
    scratch0 = [
        pltpu.VMEM((TM, D_GATE), _F32),
        pltpu.VMEM((TM, D_GATE), _BF16),
        pltpu.VMEM((TM, D_GATE), _BF16),
        pltpu.VMEM((TM, D_FF), _BF16),
    ]
    layer1_weights = (b_w_in[0], b_w_out[0], mlp_w1[1], mlp_w2[1])
    x2d, (w_in1, w_out1, w1_1, w2_1) = _run_layer(
        _layer0_kernel, x2d, consts0, scratch0, cast=layer1_weights)

    consts1 = [
        w_in1, b_conv[0], w_out1,
        _row(ln_g[1, 0]), _row(ln_b[1, 0]),
        w1_1, w2_1,
        _row(ln_g[1, 1]), _row(ln_b[1, 1]),
    ]
    scratch1 = [
        pltpu.VMEM((SUBLANES, D_MODEL), _F32),
        pltpu.VMEM((TM, D_MODEL), _BF16),
        pltpu.VMEM((TM, D_FF), _BF16),
    ]
    body1 = functools.partial(_layer1_kernel, tiles_per_seq=seq // TM)
    x2d, _ = _run_layer(body1, x2d, consts1, scratch1)
    return x2d.reshape(bsz, seq, d)
```

```python
import functools

import jax
import jax.numpy as jnp
from jax import lax
from jax.experimental import pallas as pl
from jax.experimental.pallas import tpu as pltpu

D_MODEL = 1024
D_GATE = 2 * D_MODEL
D_FF = 4 * D_MODEL
A_GROUPS = 8
A_GROUP_DIM = D_GATE // A_GROUPS
GMLP_BLOCK = 128
CHUNK = 64
CONV_WIDTH = 3
DEPTH = 2
LN_EPS = 1e-5
DEEPNORM_ALPHA = (2.0 * DEPTH) ** 0.25

SUBLANES = 8
TM = 512
SUB = 256
N_CHUNK = 512
VMEM_LIMIT_BYTES = 60 * 1024 * 1024

_BF16 = jnp.bfloat16
_F32 = jnp.float32
_SUBS = [slice(s * SUB, (s + 1) * SUB) for s in range(TM // SUB)]


def _layer_norm(z, g, b):
    mu = jnp.mean(z, axis=-1, keepdims=True)
    zc = z - mu
    var = jnp.mean(zc * zc, axis=-1, keepdims=True)
    return zc * lax.rsqrt(var + LN_EPS) * g + b


def _gelu(z):
    return 0.5 * z * (1.0 + lax.erf(z * (2.0 ** -0.5)))


def _dot(a, b):
    return jnp.dot(a, b, preferred_element_type=_F32)


def _mlp_and_norm(x1s, w1_ref, w2_ref, g2, b2, hid_scr, o_ref):
    for r, x1 in zip(_SUBS, x1s):
        x1b = x1.astype(_BF16)
        for c in range(0, D_FF, N_CHUNK):
            hid = _dot(x1b, w1_ref[:, c:c + N_CHUNK])
            hid = jnp.square(jnp.maximum(hid, 0.0))
            hid_scr[r, c:c + N_CHUNK] = hid.astype(_BF16)
    for r, x1 in zip(_SUBS, x1s):
        ff = _dot(hid_scr[r, :], w2_ref[...])
        o_ref[r, :] = _layer_norm(DEEPNORM_ALPHA * x1 + ff, g2, b2)


def _cast_slabs(refs):
    n = len(refs) // 2
    for src, dst in zip(refs[:n], refs[n:]):
        dst[...] = src[...].astype(_BF16)


def _layer0_kernel(x_ref, win_ref, bin_ref, vg_ref, vb_ref, ws_ref, bst_ref,
                   wout_ref, bout_ref, g1_ref, b1_ref, w1_ref, w2_ref, g2_ref,
                   b2_ref, *rest):
    n_cast = (len(rest) - 5) // 2
    cast_src, rest = rest[:n_cast], rest[n_cast:]
    o_ref, cast_dst = rest[0], rest[1:1 + n_cast]
    vm_scr, vn_scr, y_scr, hid_scr = rest[1 + n_cast:]
    _cast_slabs(cast_src + cast_dst)

    xs = [x_ref[r, :] for r in _SUBS]
    xbs = [x.astype(_BF16) for x in xs]

    for r, xb in zip(_SUBS, xbs):
        for c in range(0, D_GATE, N_CHUNK):
            cw = slice(D_GATE + c, D_GATE + c + N_CHUNK)
            vm_scr[r, c:c + N_CHUNK] = _gelu(_dot(xb, win_ref[:, cw]) + bin_ref[:, cw])
    for r in _SUBS:
        vn_scr[r, :] = _layer_norm(vm_scr[r, :], vg_ref[...], vb_ref[...]).astype(_BF16)

    row_chunk = lax.broadcasted_iota(jnp.int32, (GMLP_BLOCK, GMLP_BLOCK), 0) // CHUNK
    col_chunk = lax.broadcasted_iota(jnp.int32, (GMLP_BLOCK, GMLP_BLOCK), 1) // CHUNK
    mask = row_chunk >= col_chunk
    for g in range(A_GROUPS):
        cols = slice(g * A_GROUP_DIM, (g + 1) * A_GROUP_DIM)
        wg = jnp.where(mask, ws_ref[g], 0.0).astype(_BF16)
        bias = bst_ref[:, g:g + 1]
        for blk in range(TM // GMLP_BLOCK):
            rows = slice(blk * GMLP_BLOCK, (blk + 1) * GMLP_BLOCK)
            vm_scr[rows, cols] = _dot(wg, vn_scr[rows, cols]) + bias

    for r, xb in zip(_SUBS, xbs):
        for c in range(0, D_GATE, N_CHUNK):
            cw = slice(c, c + N_CHUNK)
            u = _gelu(_dot(xb, win_ref[:, cw]) + bin_ref[:, cw])
            y_scr[r, cw] = (u * vm_scr[r, cw]).astype(_BF16)

    x1s = []
    for r, x in zip(_SUBS, xs):
        mix = _dot(y_scr[r, :], wout_ref[...]) + bout_ref[...]
        x1s.append(_layer_norm(DEEPNORM_ALPHA * x + mix, g1_ref[...], b1_ref[...]))
    _mlp_and_norm(x1s, w1_ref, w2_ref, g2_ref[...], b2_ref[...], hid_scr, o_ref)


def _layer1_kernel(x_ref, win_ref, conv_ref, wout_ref, g1_ref, b1_ref, w1_ref,
                   w2_ref, g2_ref, b2_ref, o_ref, carry_scr, gate_scr, hid_scr,
                   *, tiles_per_seq):
    @pl.when(pl.program_id(0) % tiles_per_seq == 0)
    def _():
        carry_scr[...] = jnp.zeros_like(carry_scr)

    xs = [x_ref[r, :] for r in _SUBS]
    xbs = [x.astype(_BF16) for x in xs]
    prev = carry_scr[...]
    for r, xb in zip(_SUBS, xbs):
        c_gate = _dot(xb, win_ref[:, D_MODEL:2 * D_MODEL])
        hc = c_gate * _dot(xb, win_ref[:, 2 * D_MODEL:3 * D_MODEL])
        ext = jnp.concatenate([prev, hc], axis=0)
        prev = hc[-SUBLANES:, :]
        conv = conv_ref[CONV_WIDTH - 1:CONV_WIDTH, :] * hc
        for k in range(CONV_WIDTH - 1):
            shift = CONV_WIDTH - 1 - k
            conv = conv + conv_ref[k:k + 1, :] * pltpu.roll(ext, shift, axis=0)[SUBLANES:, :]
        b_gate = _dot(xb, win_ref[:, 0:D_MODEL])
        gate_scr[r, :] = (b_gate * conv).astype(_BF16)
    carry_scr[...] = prev

    x1s = []
    for r, x in zip(_SUBS, xs):
        mix = _dot(gate_scr[r, :], wout_ref[...])
        x1s.append(_layer_norm(DEEPNORM_ALPHA * x + mix, g1_ref[...], b1_ref[...]))
    _mlp_and_norm(x1s, w1_ref, w2_ref, g2_ref[...], b2_ref[...], hid_scr, o_ref)


def _resident(shape):
    zeros = (0,) * len(shape)
    return pl.BlockSpec(shape, lambda i: zeros, pipeline_mode=pl.Buffered(1))


def _layer_slab(layer):
    return lambda i: (layer, i, 0)


def _row(v):
    return v.reshape(1, -1)


def _run_layer(body, x2d, consts, scratch_shapes, cast=()):
    m = x2d.shape[0]
    steps = m // TM
    tile = pl.BlockSpec((TM, D_MODEL), lambda i: (i, 0))
    slab_in = [pl.BlockSpec((None, w.shape[1] // steps, w.shape[2]), _layer_slab(l))
               for w, l in cast]
    slab_out = [pl.BlockSpec((w.shape[1] // steps, w.shape[2]), lambda i: (i, 0)) for w, _ in cast]
    outs = pl.pallas_call(
        body,
        grid=(steps,),
        in_specs=[tile] + [_resident(c.shape) for c in consts] + slab_in,
        out_specs=[tile] + slab_out,
        out_shape=[jax.ShapeDtypeStruct((m, D_MODEL), _F32)]
        + [jax.ShapeDtypeStruct(w.shape[1:], _BF16) for w, _ in cast],
        scratch_shapes=scratch_shapes,
        compiler_params=pltpu.CompilerParams(
            dimension_semantics=("arbitrary",),
            vmem_limit_bytes=VMEM_LIMIT_BYTES),
    )(x2d, *consts, *[w for w, _ in cast])
    return outs[0], outs[1:]


def kernel(x, ln_g, ln_b, a_w_in, a_b_in, a_v_g, a_v_b, a_w_s, a_b_s, a_w_out,
           a_b_out, b_w_in, b_conv, b_w_out, mlp_w1, mlp_w2):
    bsz, seq, d = x.shape
    assert d == D_MODEL and seq % TM == 0 and SUB % GMLP_BLOCK == 0
    x2d = x.reshape(bsz * seq, d)

    consts0 = [
        a_w_in[0].astype(_BF16), _row(a_b_in[0]), _row(a_v_g[0]), _row(a_v_b[0]),
        a_w_s[0], a_b_s[0].T, a_w_out[0].astype(_BF16), _row(a_b_out[0]),
        _row(ln_g[0, 0]), _row(ln_b[0, 0]),
        mlp_w1[0].astype(_BF16), mlp_w2[0].astype(_BF16),
        _row(ln_g[0, 1]), _row(ln_b[0, 1]),
    ]
    scratch0 = [
        pltpu.VMEM((TM, D_GATE), _F32),
        pltpu.VMEM((TM, D_GATE), _BF16),
        pltpu.VMEM((TM, D_GATE), _BF16),
        pltpu.VMEM((TM, D_FF), _BF16),
    ]
    layer1_weights = ((b_w_in, 0), (b_w_out, 0), (mlp_w1, 1), (mlp_w2, 1))
    x2d, (w_in1, w_out1, w1_1, w2_1) = _run_layer(
        _layer0_kernel, x2d, consts0, scratch0, cast=layer1_weights)

    consts1 = [
        w_in1, b_conv[0], w_out1,
        _row(ln_g[1, 0]), _row(ln_b[1, 0]),
        w1_1, w2_1,
        _row(ln_g[1, 1]), _row(ln_b[1, 1]),
    ]
    scratch1 = [
        pltpu.VMEM((SUBLANES, D_MODEL), _F32),
        pltpu.VMEM((TM, D_MODEL), _BF16),
        pltpu.VMEM((TM, D_FF), _BF16),
    ]
    body1 = functools.partial(_layer1_kernel, tiles_per_seq=seq // TM)
    x2d, _ = _run_layer(body1, x2d, consts1, scratch1)
    return x2d.reshape(bsz, seq, d)
```

```python
import functools

import jax
import jax.numpy as jnp
from jax import lax
from jax.experimental import pallas as pl
from jax.experimental.pallas import tpu as pltpu

D_MODEL = 1024
D_GATE = 2 * D_MODEL
D_FF = 4 * D_MODEL
A_GROUPS = 8
A_GROUP_DIM = D_GATE // A_GROUPS
GMLP_BLOCK = 128
CHUNK = 64
CONV_WIDTH = 3
DEPTH = 2
LN_EPS = 1e-5
DEEPNORM_ALPHA = (2.0 * DEPTH) ** 0.25

SUBLANES = 8
TM = 512
SUB = 256
N_CHUNK = 512
VMEM_LIMIT_BYTES = 60 * 1024 * 1024

_BF16 = jnp.bfloat16
_F32 = jnp.float32
_SUBS = [slice(s * SUB, (s + 1) * SUB) for s in range(TM // SUB)]


def _layer_norm(z, g, b):
    mu = jnp.mean(z, axis=-1, keepdims=True)
    zc = z - mu
    var = jnp.mean(zc * zc, axis=-1, keepdims=True)
    return zc * lax.rsqrt(var + LN_EPS) * g + b


def _gelu(z):
    return 0.5 * z * (1.0 + lax.erf(z * (2.0 ** -0.5)))


def _dot(a, b):
    return jnp.dot(a, b, preferred_element_type=_F32)


def _weight(ref, cols=slice(None)):
    w = ref[:, cols]
    return pltpu.bitcast(w, _BF16) if w.dtype == jnp.uint32 else w


def _mlp_and_norm(x1s, w1_ref, w2_ref, g2, b2, hid_scr, o_ref):
    for r, x1 in zip(_SUBS, x1s):
        x1b = x1.astype(_BF16)
        for c in range(0, D_FF, N_CHUNK):
            hid = _dot(x1b, _weight(w1_ref, slice(c, c + N_CHUNK)))
            hid = jnp.square(jnp.maximum(hid, 0.0))
            hid_scr[r, c:c + N_CHUNK] = hid.astype(_BF16)
    for r, x1 in zip(_SUBS, x1s):
        ff = _dot(hid_scr[r, :], _weight(w2_ref))
        o_ref[r, :] = _layer_norm(DEEPNORM_ALPHA * x1 + ff, g2, b2)


def _cast_slabs(refs):
    n = len(refs) // 2
    for src, dst in zip(refs[:n], refs[n:]):
        dst[...] = pltpu.bitcast(src[...].astype(_BF16), jnp.uint32)


def _layer0_kernel(x_ref, win_ref, bin_ref, vg_ref, vb_ref, ws_ref, bst_ref,
                   wout_ref, bout_ref, g1_ref, b1_ref, w1_ref, w2_ref, g2_ref,
                   b2_ref, *rest):
    n_cast = (len(rest) - 5) // 2
    cast_src, rest = rest[:n_cast], rest[n_cast:]
    o_ref, cast_dst = rest[0], rest[1:1 + n_cast]
    vm_scr, vn_scr, y_scr, hid_scr = rest[1 + n_cast:]
    _cast_slabs(cast_src + cast_dst)

    xs = [x_ref[r, :] for r in _SUBS]
    xbs = [x.astype(_BF16) for x in xs]

    for r, xb in zip(_SUBS, xbs):
        for c in range(0, D_GATE, N_CHUNK):
            cw = slice(D_GATE + c, D_GATE + c + N_CHUNK)
            vm_scr[r, c:c + N_CHUNK] = _gelu(_dot(xb, _weight(win_ref, cw)) + bin_ref[:, cw])
    for r in _SUBS:
        vn_scr[r, :] = _layer_norm(vm_scr[r, :], vg_ref[...], vb_ref[...]).astype(_BF16)

    row_chunk = lax.broadcasted_iota(jnp.int32, (GMLP_BLOCK, GMLP_BLOCK), 0) // CHUNK
    col_chunk = lax.broadcasted_iota(jnp.int32, (GMLP_BLOCK, GMLP_BLOCK), 1) // CHUNK
    mask = row_chunk >= col_chunk
    for g in range(A_GROUPS):
        cols = slice(g * A_GROUP_DIM, (g + 1) * A_GROUP_DIM)
        wg = jnp.where(mask, ws_ref[g], 0.0).astype(_BF16)
        bias = bst_ref[:, g:g + 1]
        for blk in range(TM // GMLP_BLOCK):
            rows = slice(blk * GMLP_BLOCK, (blk + 1) * GMLP_BLOCK)
            vm_scr[rows, cols] = _dot(wg, vn_scr[rows, cols]) + bias

    for r, xb in zip(_SUBS, xbs):
        for c in range(0, D_GATE, N_CHUNK):
            cw = slice(c, c + N_CHUNK)
            u = _gelu(_dot(xb, _weight(win_ref, cw)) + bin_ref[:, cw])
            y_scr[r, cw] = (u * vm_scr[r, cw]).astype(_BF16)

    x1s = []
    for r, x in zip(_SUBS, xs):
        mix = _dot(y_scr[r, :], _weight(wout_ref)) + bout_ref[...]
        x1s.append(_layer_norm(DEEPNORM_ALPHA * x + mix, g1_ref[...], b1_ref[...]))
    _mlp_and_norm(x1s, w1_ref, w2_ref, g2_ref[...], b2_ref[...], hid_scr, o_ref)


def _layer1_kernel(x_ref, win_ref, conv_ref, wout_ref, g1_ref, b1_ref, w1_ref,
                   w2_ref, g2_ref, b2_ref, o_ref, carry_scr, gate_scr, hid_scr,
                   *, tiles_per_seq):
    @pl.when(pl.program_id(0) % tiles_per_seq == 0)
    def _():
        carry_scr[...] = jnp.zeros_like(carry_scr)

    xs = [x_ref[r, :] for r in _SUBS]
    xbs = [x.astype(_BF16) for x in xs]
    prev = carry_scr[...]
    for r, xb in zip(_SUBS, xbs):
        c_gate = _dot(xb, _weight(win_ref, slice(D_MODEL, 2 * D_MODEL)))
        hc = c_gate * _dot(xb, _weight(win_ref, slice(2 * D_MODEL, 3 * D_MODEL)))
        ext = jnp.concatenate([prev, hc], axis=0)
        prev = hc[-SUBLANES:, :]
        conv = conv_ref[CONV_WIDTH - 1:CONV_WIDTH, :] * hc
        for k in range(CONV_WIDTH - 1):
            shift = CONV_WIDTH - 1 - k
            conv = conv + conv_ref[k:k + 1, :] * pltpu.roll(ext, shift, axis=0)[SUBLANES:, :]
        b_gate = _dot(xb, _weight(win_ref, slice(0, D_MODEL)))
        gate_scr[r, :] = (b_gate * conv).astype(_BF16)
    carry_scr[...] = prev

    x1s = []
    for r, x in zip(_SUBS, xs):
        mix = _dot(gate_scr[r, :], _weight(wout_ref))
        x1s.append(_layer_norm(DEEPNORM_ALPHA * x + mix, g1_ref[...], b1_ref[...]))
    _mlp_and_norm(x1s, w1_ref, w2_ref, g2_ref[...], b2_ref[...], hid_scr, o_ref)


def _resident(shape):
    zeros = (0,) * len(shape)
    return pl.BlockSpec(shape, lambda i: zeros, pipeline_mode=pl.Buffered(1))


def _layer_slab(layer):
    return lambda i: (layer, i, 0)


def _row(v):
    return v.reshape(1, -1)


def _run_layer(body, x2d, consts, scratch_shapes, cast=()):
    m = x2d.shape[0]
    steps = m // TM
    tile = pl.BlockSpec((TM, D_MODEL), lambda i: (i, 0))
    slab_in = [pl.BlockSpec((None, w.shape[1] // steps, w.shape[2]), _layer_slab(l))
               for w, l in cast]
    slab_out = [pl.BlockSpec((w.shape[1] // steps // 2, w.shape[2]), lambda i: (i, 0))
                for w, _ in cast]
    outs = pl.pallas_call(
        body,
        grid=(steps,),
        in_specs=[tile] + [_resident(c.shape) for c in consts] + slab_in,
        out_specs=[tile] + slab_out,
        out_shape=[jax.ShapeDtypeStruct((m, D_MODEL), _F32)]
        + [jax.ShapeDtypeStruct((w.shape[1] // 2, w.shape[2]), jnp.uint32) for w, _ in cast],
        scratch_shapes=scratch_shapes,
        compiler_params=pltpu.CompilerParams(
            dimension_semantics=("arbitrary",),
            vmem_limit_bytes=VMEM_LIMIT_BYTES),
    )(x2d, *consts, *[w for w, _ in cast])
    return outs[0], outs[1:]


def kernel(x, ln_g, ln_b, a_w_in, a_b_in, a_v_g, a_v_b, a_w_s, a_b_s, a_w_out,
           a_b_out, b_w_in, b_conv, b_w_out, mlp_w1, mlp_w2):
    bsz, seq, d = x.shape
    assert d == D_MODEL and seq % TM == 0 and SUB % GMLP_BLOCK == 0
    x2d = x.reshape(bsz * seq, d)

    consts0 = [
        a_w_in[0].astype(_BF16), _row(a_b_in[0]), _row(a_v_g[0]), _row(a_v_b[0]),
        a_w_s[0], a_b_s[0].T, a_w_out[0].astype(_BF16), _row(a_b_out[0]),
        _row(ln_g[0, 0]), _row(ln_b[0, 0]),
        mlp_w1[0].astype(_BF16), mlp_w2[0].astype(_BF16),
        _row(ln_g[0, 1]), _row(ln_b[0, 1]),
    ]
    scratch0 = [
        pltpu.VMEM((TM, D_GATE), _F32),
        pltpu.VMEM((TM, D_GATE), _BF16),
        pltpu.VMEM((TM, D_GATE), _BF16),
        pltpu.VMEM((TM, D_FF), _BF16),
    ]
    layer1_weights = ((b_w_in, 0), (b_w_out, 0), (mlp_w1, 1), (mlp_w2, 1))
    x2d, (w_in1, w_out1, w1_1, w2_1) = _run_layer(
        _layer0_kernel, x2d, consts0, scratch0, cast=layer1_weights)

    consts1 = [
        w_in1, b_conv[0], w_out1,
        _row(ln_g[1, 0]), _row(ln_b[1, 0]),
        w1_1, w2_1,
        _row(ln_g[1, 1]), _row(ln_b[1, 1]),
    ]
    scratch1 = [
        pltpu.VMEM((SUBLANES, D_MODEL), _F32),
        pltpu.VMEM((TM, D_MODEL), _BF16),
        pltpu.VMEM((TM, D_FF), _BF16),
    ]
    body1 = functools.partial(_layer1_kernel, tiles_per_seq=seq // TM)
    x2d, _ = _run_layer(body1, x2d, consts1, scratch1)
    return x2d.reshape(bsz, seq, d)
```

```python
import functools

import jax
import jax.numpy as jnp
from jax import lax
from jax.experimental import pallas as pl
from jax.experimental.pallas import tpu as pltpu

D_MODEL = 1024
D_GATE = 2 * D_MODEL
D_FF = 4 * D_MODEL
A_GROUPS = 8
A_GROUP_DIM = D_GATE // A_GROUPS
GMLP_BLOCK = 128
CHUNK = 64
CONV_WIDTH = 3
DEPTH = 2
LN_EPS = 1e-5
DEEPNORM_ALPHA = (2.0 * DEPTH) ** 0.25

SUBLANES = 8
TM = 512
SUB = 256
N_CHUNK = 512
STAGE_ROWS, STAGE_COLS = 128, 1024
N_STAGE = 8
VMEM_LIMIT_BYTES = 62 * 1024 * 1024

_BF16 = jnp.bfloat16
_F32 = jnp.float32
_SUBS = [slice(s * SUB, (s + 1) * SUB) for s in range(TM // SUB)]


def _layer_norm(z, g, b):
    mu = jnp.mean(z, axis=-1, keepdims=True)
    zc = z - mu
    var = jnp.mean(zc * zc, axis=-1, keepdims=True)
    return zc * lax.rsqrt(var + LN_EPS) * g + b


def _gelu(z):
    return 0.5 * z * (1.0 + lax.erf(z * (2.0 ** -0.5)))


def _dot(a, b):
    return jnp.dot(a, b, preferred_element_type=_F32)


def _weight(ref, cols=slice(None)):
    w = ref[:, cols]
    return pltpu.bitcast(w, _BF16) if w.dtype == jnp.uint32 else w


def _mlp_and_norm(x1s, w1_ref, w2_ref, g2, b2, hid_scr, o_ref):
    for r, x1 in zip(_SUBS, x1s):
        x1b = x1.astype(_BF16)
        for c in range(0, D_FF, N_CHUNK):
            hid = _dot(x1b, _weight(w1_ref, slice(c, c + N_CHUNK)))
            hid = jnp.square(jnp.maximum(hid, 0.0))
            hid_scr[r, c:c + N_CHUNK] = hid.astype(_BF16)
    for r, x1 in zip(_SUBS, x1s):
        ff = _dot(hid_scr[r, :], _weight(w2_ref))
        o_ref[r, :] = _layer_norm(DEEPNORM_ALPHA * x1 + ff, g2, b2)


def _cast_slabs(refs):
    n = len(refs) // 2
    for src, dst in zip(refs[:n], refs[n:]):
        dst[...] = pltpu.bitcast(src[...].astype(_BF16), jnp.uint32)


def _load_weights_bf16(srcs, dsts, stage, sems):
    chunks = []
    for src, dst in zip(srcs, dsts):
        for r0 in range(0, dst.shape[0], STAGE_ROWS):
            for c0 in range(0, dst.shape[1], STAGE_COLS):
                chunks.append((src, dst, r0, c0))

    def copy(j):
        src, _, r0, c0 = chunks[j]
        slot = j % N_STAGE
        return pltpu.make_async_copy(
            src.at[0, pl.ds(r0, STAGE_ROWS), pl.ds(c0, STAGE_COLS)],
            stage.at[slot], sems.at[slot])

    for j in range(min(N_STAGE, len(chunks))):
        copy(j).start()
    for j, (_, dst, r0, c0) in enumerate(chunks):
        copy(j).wait()
        dst[r0:r0 + STAGE_ROWS, c0:c0 + STAGE_COLS] = stage[j % N_STAGE].astype(_BF16)
        if j + N_STAGE < len(chunks):
            copy(j + N_STAGE).start()


def _layer0_kernel(x_ref, bin_ref, vg_ref, vb_ref, ws_ref, bst_ref, bout_ref,
                   g1_ref, b1_ref, g2_ref, b2_ref,
                   win_hbm, wout_hbm, w1_hbm, w2_hbm,
                   cast0, cast1, cast2, cast3,
                   o_ref, cast0_out, cast1_out, cast2_out, cast3_out,
                   win_ref, wout_ref, w1_ref, w2_ref, stage, sems,
                   vm_scr, vn_scr, y_scr, hid_scr):
    @pl.when(pl.program_id(0) == 0)
    def _():
        _load_weights_bf16((win_hbm, wout_hbm, w1_hbm, w2_hbm),
                           (win_ref, wout_ref, w1_ref, w2_ref), stage, sems)

    _cast_slabs((cast0, cast1, cast2, cast3, cast0_out, cast1_out, cast2_out, cast3_out))

    xs = [x_ref[r, :] for r in _SUBS]
    xbs = [x.astype(_BF16) for x in xs]

    for r, xb in zip(_SUBS, xbs):
        for c in range(0, D_GATE, N_CHUNK):
            cw = slice(D_GATE + c, D_GATE + c + N_CHUNK)
            vm_scr[r, c:c + N_CHUNK] = _gelu(_dot(xb, _weight(win_ref, cw)) + bin_ref[:, cw])
    for r in _SUBS:
        vn_scr[r, :] = _layer_norm(vm_scr[r, :], vg_ref[...], vb_ref[...]).astype(_BF16)

    row_chunk = lax.broadcasted_iota(jnp.int32, (GMLP_BLOCK, GMLP_BLOCK), 0) // CHUNK
    col_chunk = lax.broadcasted_iota(jnp.int32, (GMLP_BLOCK, GMLP_BLOCK), 1) // CHUNK
    mask = row_chunk >= col_chunk
    for g in range(A_GROUPS):
        cols = slice(g * A_GROUP_DIM, (g + 1) * A_GROUP_DIM)
        wg = jnp.where(mask, ws_ref[g], 0.0).astype(_BF16)
        bias = bst_ref[:, g:g + 1]
        for blk in range(TM // GMLP_BLOCK):
            rows = slice(blk * GMLP_BLOCK, (blk + 1) * GMLP_BLOCK)
            vm_scr[rows, cols] = _dot(wg, vn_scr[rows, cols]) + bias

    for r, xb in zip(_SUBS, xbs):
        for c in range(0, D_GATE, N_CHUNK):
            cw = slice(c, c + N_CHUNK)
            u = _gelu(_dot(xb, _weight(win_ref, cw)) + bin_ref[:, cw])
            y_scr[r, cw] = (u * vm_scr[r, cw]).astype(_BF16)

    x1s = []
    for r, x in zip(_SUBS, xs):
        mix = _dot(y_scr[r, :], _weight(wout_ref)) + bout_ref[...]
        x1s.append(_layer_norm(DEEPNORM_ALPHA * x + mix, g1_ref[...], b1_ref[...]))
    _mlp_and_norm(x1s, w1_ref, w2_ref, g2_ref[...], b2_ref[...], hid_scr, o_ref)


def _layer1_kernel(x_ref, win_ref, conv_ref, wout_ref, g1_ref, b1_ref, w1_ref,
                   w2_ref, g2_ref, b2_ref, o_ref, carry_scr, gate_scr, hid_scr,
                   *, tiles_per_seq):
    @pl.when(pl.program_id(0) % tiles_per_seq == 0)
    def _():
        carry_scr[...] = jnp.zeros_like(carry_scr)

    xs = [x_ref[r, :] for r in _SUBS]
    xbs = [x.astype(_BF16) for x in xs]
    prev = carry_scr[...]
    for r, xb in zip(_SUBS, xbs):
        c_gate = _dot(xb, _weight(win_ref, slice(D_MODEL, 2 * D_MODEL)))
        hc = c_gate * _dot(xb, _weight(win_ref, slice(2 * D_MODEL, 3 * D_MODEL)))
        ext = jnp.concatenate([prev, hc], axis=0)
        prev = hc[-SUBLANES:, :]
        conv = conv_ref[CONV_WIDTH - 1:CONV_WIDTH, :] * hc
        for k in range(CONV_WIDTH - 1):
            shift = CONV_WIDTH - 1 - k
            conv = conv + conv_ref[k:k + 1, :] * pltpu.roll(ext, shift, axis=0)[SUBLANES:, :]
        b_gate = _dot(xb, _weight(win_ref, slice(0, D_MODEL)))
        gate_scr[r, :] = (b_gate * conv).astype(_BF16)
    carry_scr[...] = prev

    x1s = []
    for r, x in zip(_SUBS, xs):
        mix = _dot(gate_scr[r, :], _weight(wout_ref))
        x1s.append(_layer_norm(DEEPNORM_ALPHA * x + mix, g1_ref[...], b1_ref[...]))
    _mlp_and_norm(x1s, w1_ref, w2_ref, g2_ref[...], b2_ref[...], hid_scr, o_ref)


def _resident(shape):
    zeros = (0,) * len(shape)
    return pl.BlockSpec(shape, lambda i: zeros, pipeline_mode=pl.Buffered(1))


def _layer_slab(layer):
    return lambda i: (layer, i, 0)


def _row(v):
    return v.reshape(1, -1)


def _run_layer(body, x2d, consts, scratch_shapes, hbm=(), cast=()):
    m = x2d.shape[0]
    steps = m // TM
    tile = pl.BlockSpec((TM, D_MODEL), lambda i: (i, 0))
    slab_in = [pl.BlockSpec((None, w.shape[1] // steps, w.shape[2]), _layer_slab(l))
               for w, l in cast]
    slab_out = [pl.BlockSpec((w.shape[1] // steps // 2, w.shape[2]), lambda i: (i, 0))
                for w, _ in cast]
    outs = pl.pallas_call(
        body,
        grid=(steps,),
        in_specs=[tile] + [_resident(c.shape) for c in consts]
        + [pl.BlockSpec(memory_space=pl.ANY)] * len(hbm) + slab_in,
        out_specs=[tile] + slab_out,
        out_shape=[jax.ShapeDtypeStruct((m, D_MODEL), _F32)]
        + [jax.ShapeDtypeStruct((w.shape[1] // 2, w.shape[2]), jnp.uint32) for w, _ in cast],
        scratch_shapes=scratch_shapes,
        compiler_params=pltpu.CompilerParams(
            dimension_semantics=("arbitrary",),
            vmem_limit_bytes=VMEM_LIMIT_BYTES),
    )(x2d, *consts, *hbm, *[w for w, _ in cast])
    return outs[0], outs[1:]


def kernel(x, ln_g, ln_b, a_w_in, a_b_in, a_v_g, a_v_b, a_w_s, a_b_s, a_w_out,
           a_b_out, b_w_in, b_conv, b_w_out, mlp_w1, mlp_w2):
    bsz, seq, d = x.shape
    assert d == D_MODEL and seq % TM == 0 and SUB % GMLP_BLOCK == 0
    x2d = x.reshape(bsz * seq, d)

    consts0 = [
        _row(a_b_in[0]), _row(a_v_g[0]), _row(a_v_b[0]), a_w_s[0], a_b_s[0].T,
        _row(a_b_out[0]), _row(ln_g[0, 0]), _row(ln_b[0, 0]),
        _row(ln_g[0, 1]), _row(ln_b[0, 1]),
    ]
    layer0_weights = (a_w_in, a_w_out, mlp_w1, mlp_w2)
    scratch0 = [
        pltpu.VMEM((D_MODEL, 2 * D_GATE), _BF16),
        pltpu.VMEM((D_GATE, D_MODEL), _BF16),
        pltpu.VMEM((D_MODEL, D_FF), _BF16),
        pltpu.VMEM((D_FF, D_MODEL), _BF16),
        pltpu.VMEM((N_STAGE, STAGE_ROWS, STAGE_COLS), _F32),
        pltpu.SemaphoreType.DMA((N_STAGE,)),
        pltpu.VMEM((TM, D_GATE), _F32),
        pltpu.VMEM((TM, D_GATE), _BF16),
        pltpu.VMEM((TM, D_GATE), _BF16),
        pltpu.VMEM((TM, D_FF), _BF16),
    ]
    layer1_weights = ((b_w_in, 0), (b_w_out, 0), (mlp_w1, 1), (mlp_w2, 1))
    x2d, (w_in1, w_out1, w1_1, w2_1) = _run_layer(
        _layer0_kernel, x2d, consts0, scratch0, hbm=layer0_weights, cast=layer1_weights)

    consts1 = [
        w_in1, b_conv[0], w_out1,
        _row(ln_g[1, 0]), _row(ln_b[1, 0]),
        w1_1, w2_1,
        _row(ln_g[1, 1]), _row(ln_b[1, 1]),
    ]
    scratch1 = [
        pltpu.VMEM((SUBLANES, D_MODEL), _F32),
        pltpu.VMEM((TM, D_MODEL), _BF16),
        pltpu.VMEM((TM, D_FF), _BF16),
    ]
    body1 = functools.partial(_layer1_kernel, tiles_per_seq=seq // TM)
    x2d, _ = _run_layer(body1, x2d, consts1, scratch1)
    return x2d.reshape(bsz, seq, d)
```

```python
import functools

import jax
import jax.numpy as jnp
from jax import lax
from jax.experimental import pallas as pl
from jax.experimental.pallas import tpu as pltpu

D_MODEL = 1024
D_GATE = 2 * D_MODEL
D_FF = 4 * D_MODEL
A_GROUPS = 8
A_GROUP_DIM = D_GATE // A_GROUPS
GMLP_BLOCK = 128
CHUNK = 64
CONV_WIDTH = 3
DEPTH = 2
LN_EPS = 1e-5
DEEPNORM_ALPHA = (2.0 * DEPTH) ** 0.25

SUBLANES = 8
TM_LAYER0 = 512
TM_LAYER1 = 1024
SUB = 256
N_CHUNK = 512
STAGE_ROWS, STAGE_COLS = 128, 1024
N_STAGE = 8
VMEM_LIMIT_BYTES = 62 * 1024 * 1024

_BF16 = jnp.bfloat16
_F32 = jnp.float32


def _layer_norm(z, g, b):
    mu = jnp.mean(z, axis=-1, keepdims=True)
    zc = z - mu
    var = jnp.mean(zc * zc, axis=-1, keepdims=True)
    return zc * lax.rsqrt(var + LN_EPS) * g + b


def _gelu(z):
    return 0.5 * z * (1.0 + lax.erf(z * (2.0 ** -0.5)))


def _dot(a, b):
    return jnp.dot(a, b, preferred_element_type=_F32)


def _sub_tiles(ref):
    return [slice(r0, r0 + SUB) for r0 in range(0, ref.shape[0], SUB)]


def _weight(ref, cols=slice(None)):
    w = ref[:, cols]
    return pltpu.bitcast(w, _BF16) if w.dtype == jnp.uint32 else w


def _mlp_and_norm(x1s, w1_ref, w2_ref, g2, b2, hid_scr, o_ref):
    subs = _sub_tiles(o_ref)
    for r, x1 in zip(subs, x1s):
        x1b = x1.astype(_BF16)
        for c in range(0, D_FF, N_CHUNK):
            hid = _dot(x1b, _weight(w1_ref, slice(c, c + N_CHUNK)))
            hid = jnp.square(jnp.maximum(hid, 0.0))
            hid_scr[r, c:c + N_CHUNK] = hid.astype(_BF16)
    for r, x1 in zip(subs, x1s):
        ff = _dot(hid_scr[r, :], _weight(w2_ref))
        o_ref[r, :] = _layer_norm(DEEPNORM_ALPHA * x1 + ff, g2, b2)


def _cast_slabs(refs):
    n = len(refs) // 2
    for src, dst in zip(refs[:n], refs[n:]):
        dst[...] = pltpu.bitcast(src[...].astype(_BF16), jnp.uint32)


def _load_weights_bf16(srcs, dsts, stage, sems):
    chunks = []
    for src, dst in zip(srcs, dsts):
        for r0 in range(0, dst.shape[0], STAGE_ROWS):
            for c0 in range(0, dst.shape[1], STAGE_COLS):
                chunks.append((src, dst, r0, c0))

    def copy(j):
        src, _, r0, c0 = chunks[j]
        slot = j % N_STAGE
        return pltpu.make_async_copy(
            src.at[0, pl.ds(r0, STAGE_ROWS), pl.ds(c0, STAGE_COLS)],
            stage.at[slot], sems.at[slot])

    for j in range(min(N_STAGE, len(chunks))):
        copy(j).start()
    for j, (_, dst, r0, c0) in enumerate(chunks):
        copy(j).wait()
        dst[r0:r0 + STAGE_ROWS, c0:c0 + STAGE_COLS] = stage[j % N_STAGE].astype(_BF16)
        if j + N_STAGE < len(chunks):
            copy(j + N_STAGE).start()


def _layer0_kernel(x_ref, bin_ref, vg_ref, vb_ref, ws_ref, bst_ref, bout_ref,
                   g1_ref, b1_ref, g2_ref, b2_ref,
                   win_hbm, wout_hbm, w1_hbm, w2_hbm,
                   cast0, cast1, cast2, cast3,
                   o_ref, cast0_out, cast1_out, cast2_out, cast3_out,
                   win_ref, wout_ref, w1_ref, w2_ref, stage, sems,
                   vm_scr, vn_scr, y_scr, hid_scr):
    @pl.when(pl.program_id(0) == 0)
    def _():
        _load_weights_bf16((win_hbm, wout_hbm, w1_hbm, w2_hbm),
                           (win_ref, wout_ref, w1_ref, w2_ref), stage, sems)

    _cast_slabs((cast0, cast1, cast2, cast3, cast0_out, cast1_out, cast2_out, cast3_out))

    subs = _sub_tiles(x_ref)
    xs = [x_ref[r, :] for r in subs]
    xbs = [x.astype(_BF16) for x in xs]

    for r, xb in zip(subs, xbs):
        for c in range(0, D_GATE, N_CHUNK):
            cw = slice(D_GATE + c, D_GATE + c + N_CHUNK)
            vm_scr[r, c:c + N_CHUNK] = _gelu(_dot(xb, _weight(win_ref, cw)) + bin_ref[:, cw])
    for r in subs:
        vn_scr[r, :] = _layer_norm(vm_scr[r, :], vg_ref[...], vb_ref[...]).astype(_BF16)

    row_chunk = lax.broadcasted_iota(jnp.int32, (GMLP_BLOCK, GMLP_BLOCK), 0) // CHUNK
    col_chunk = lax.broadcasted_iota(jnp.int32, (GMLP_BLOCK, GMLP_BLOCK), 1) // CHUNK
    mask = row_chunk >= col_chunk
    for g in range(A_GROUPS):
        cols = slice(g * A_GROUP_DIM, (g + 1) * A_GROUP_DIM)
        wg = jnp.where(mask, ws_ref[g], 0.0).astype(_BF16)
        bias = bst_ref[:, g:g + 1]
        for blk in range(x_ref.shape[0] // GMLP_BLOCK):
            rows = slice(blk * GMLP_BLOCK, (blk + 1) * GMLP_BLOCK)
            vm_scr[rows, cols] = _dot(wg, vn_scr[rows, cols]) + bias

    for r, xb in zip(subs, xbs):
        for c in range(0, D_GATE, N_CHUNK):
            cw = slice(c, c + N_CHUNK)
            u = _gelu(_dot(xb, _weight(win_ref, cw)) + bin_ref[:, cw])
            y_scr[r, cw] = (u * vm_scr[r, cw]).astype(_BF16)

    x1s = []
    for r, x in zip(subs, xs):
        mix = _dot(y_scr[r, :], _weight(wout_ref)) + bout_ref[...]
        x1s.append(_layer_norm(DEEPNORM_ALPHA * x + mix, g1_ref[...], b1_ref[...]))
    _mlp_and_norm(x1s, w1_ref, w2_ref, g2_ref[...], b2_ref[...], hid_scr, o_ref)


def _layer1_kernel(x_ref, win_ref, conv_ref, wout_ref, g1_ref, b1_ref, w1_ref,
                   w2_ref, g2_ref, b2_ref, o_ref, carry_scr, gate_scr, hid_scr,
                   *, tiles_per_seq):
    @pl.when(pl.program_id(0) % tiles_per_seq == 0)
    def _():
        carry_scr[...] = jnp.zeros_like(carry_scr)

    subs = _sub_tiles(x_ref)
    xs = [x_ref[r, :] for r in subs]
    xbs = [x.astype(_BF16) for x in xs]
    prev = carry_scr[...]
    for r, xb in zip(subs, xbs):
        c_gate = _dot(xb, _weight(win_ref, slice(D_MODEL, 2 * D_MODEL)))
        hc = c_gate * _dot(xb, _weight(win_ref, slice(2 * D_MODEL, 3 * D_MODEL)))
        ext = jnp.concatenate([prev, hc], axis=0)
        prev = hc[-SUBLANES:, :]
        conv = conv_ref[CONV_WIDTH - 1:CONV_WIDTH, :] * hc
        for k in range(CONV_WIDTH - 1):
            shift = CONV_WIDTH - 1 - k
            conv = conv + conv_ref[k:k + 1, :] * pltpu.roll(ext, shift, axis=0)[SUBLANES:, :]
        b_gate = _dot(xb, _weight(win_ref, slice(0, D_MODEL)))
        gate_scr[r, :] = (b_gate * conv).astype(_BF16)
    carry_scr[...] = prev

    x1s = []
    for r, x in zip(subs, xs):
        mix = _dot(gate_scr[r, :], _weight(wout_ref))
        x1s.append(_layer_norm(DEEPNORM_ALPHA * x + mix, g1_ref[...], b1_ref[...]))
    _mlp_and_norm(x1s, w1_ref, w2_ref, g2_ref[...], b2_ref[...], hid_scr, o_ref)


def _resident(shape):
    zeros = (0,) * len(shape)
    return pl.BlockSpec(shape, lambda i: zeros, pipeline_mode=pl.Buffered(1))


def _layer_slab(layer):
    return lambda i: (layer, i, 0)


def _row(v):
    return v.reshape(1, -1)


def _run_layer(body, x2d, tm, consts, scratch_shapes, hbm=(), cast=()):
    m = x2d.shape[0]
    steps = m // tm
    tile = pl.BlockSpec((tm, D_MODEL), lambda i: (i, 0))
    slab_in = [pl.BlockSpec((None, w.shape[1] // steps, w.shape[2]), _layer_slab(l))
               for w, l in cast]
    slab_out = [pl.BlockSpec((w.shape[1] // steps // 2, w.shape[2]), lambda i: (i, 0))
                for w, _ in cast]
    outs = pl.pallas_call(
        body,
        grid=(steps,),
        in_specs=[tile] + [_resident(c.shape) for c in consts]
        + [pl.BlockSpec(memory_space=pl.ANY)] * len(hbm) + slab_in,
        out_specs=[tile] + slab_out,
        out_shape=[jax.ShapeDtypeStruct((m, D_MODEL), _F32)]
        + [jax.ShapeDtypeStruct((w.shape[1] // 2, w.shape[2]), jnp.uint32) for w, _ in cast],
        scratch_shapes=scratch_shapes,
        compiler_params=pltpu.CompilerParams(
            dimension_semantics=("arbitrary",),
            vmem_limit_bytes=VMEM_LIMIT_BYTES),
    )(x2d, *consts, *hbm, *[w for w, _ in cast])
    return outs[0], outs[1:]


def kernel(x, ln_g, ln_b, a_w_in, a_b_in, a_v_g, a_v_b, a_w_s, a_b_s, a_w_out,
           a_b_out, b_w_in, b_conv, b_w_out, mlp_w1, mlp_w2):
    bsz, seq, d = x.shape
    assert d == D_MODEL and seq % TM_LAYER0 == 0 and seq % TM_LAYER1 == 0
    assert TM_LAYER0 % SUB == 0 and TM_LAYER1 % SUB == 0 and SUB % GMLP_BLOCK == 0
    x2d = x.reshape(bsz * seq, d)

    consts0 = [
        _row(a_b_in[0]), _row(a_v_g[0]), _row(a_v_b[0]), a_w_s[0], a_b_s[0].T,
        _row(a_b_out[0]), _row(ln_g[0, 0]), _row(ln_b[0, 0]),
        _row(ln_g[0, 1]), _row(ln_b[0, 1]),
    ]
    layer0_weights = (a_w_in, a_w_out, mlp_w1, mlp_w2)
    scratch0 = [
        pltpu.VMEM((D_MODEL, 2 * D_GATE), _BF16),
        pltpu.VMEM((D_GATE, D_MODEL), _BF16),
        pltpu.VMEM((D_MODEL, D_FF), _BF16),
        pltpu.VMEM((D_FF, D_MODEL), _BF16),
        pltpu.VMEM((N_STAGE, STAGE_ROWS, STAGE_COLS), _F32),
        pltpu.SemaphoreType.DMA((N_STAGE,)),
        pltpu.VMEM((TM_LAYER0, D_GATE), _F32),
        pltpu.VMEM((TM_LAYER0, D_GATE), _BF16),
        pltpu.VMEM((TM_LAYER0, D_GATE), _BF16),
        pltpu.VMEM((TM_LAYER0, D_FF), _BF16),
    ]
    layer1_weights = ((b_w_in, 0), (b_w_out, 0), (mlp_w1, 1), (mlp_w2, 1))
    x2d, (w_in1, w_out1, w1_1, w2_1) = _run_layer(
        _layer0_kernel, x2d, TM_LAYER0, consts0, scratch0,
        hbm=layer0_weights, cast=layer1_weights)

    consts1 = [
        w_in1, b_conv[0], w_out1,
        _row(ln_g[1, 0]), _row(ln_b[1, 0]),
        w1_1, w2_1,
        _row(ln_g[1, 1]), _row(ln_b[1, 1]),
    ]
    scratch1 = [
        pltpu.VMEM((SUBLANES, D_MODEL), _F32),
        pltpu.VMEM((TM_LAYER1, D_MODEL), _BF16),
        pltpu.VMEM((TM_LAYER1, D_FF), _BF16),
    ]
    body1 = functools.partial(_layer1_kernel, tiles_per_seq=seq // TM_LAYER1)
    x2d, _ = _run_layer(body1, x2d, TM_LAYER1, consts1, scratch1)
    return x2d.reshape(bsz, seq, d)
```

```python
import functools

import jax
import jax.numpy as jnp
from jax import lax
from jax.experimental import pallas as pl
from jax.experimental.pallas import tpu as pltpu

D_MODEL = 1024
D_GATE = 2 * D_MODEL
D_FF = 4 * D_MODEL
A_GROUPS = 8
A_GROUP_DIM = D_GATE // A_GROUPS
GMLP_BLOCK = 128
CHUNK = 64
CONV_WIDTH = 3
DEPTH = 2
LN_EPS = 1e-5
DEEPNORM_ALPHA = (2.0 * DEPTH) ** 0.25

SUBLANES = 8
TM_LAYER0 = 512
TM_LAYER1 = 512
SUB = 256
N_CHUNK = 512
STAGE_ROWS, STAGE_COLS = 128, 1024
N_STAGE = 8
VMEM_LIMIT_BYTES = 62 * 1024 * 1024

_BF16 = jnp.bfloat16
_F32 = jnp.float32


def _layer_norm(z, g, b):
    mu = jnp.mean(z, axis=-1, keepdims=True)
    zc = z - mu
    var = jnp.mean(zc * zc, axis=-1, keepdims=True)
    return zc * lax.rsqrt(var + LN_EPS) * g + b


def _gelu(z):
    return 0.5 * z * (1.0 + lax.erf(z * (2.0 ** -0.5)))


def _dot(a, b):
    return jnp.dot(a, b, preferred_element_type=_F32)


def _sub_tiles(ref):
    return [slice(r0, r0 + SUB) for r0 in range(0, ref.shape[0], SUB)]


def _weight(ref, cols=slice(None)):
    w = ref[:, cols]
    return pltpu.bitcast(w, _BF16) if w.dtype == jnp.uint32 else w


def _mlp(x1s, w1_ref, w2_ref, hid_scr, z_scr):
    subs = _sub_tiles(z_scr)
    for r, x1 in zip(subs, x1s):
        x1b = x1.astype(_BF16)
        for c in range(0, D_FF, N_CHUNK):
            hid = _dot(x1b, _weight(w1_ref, slice(c, c + N_CHUNK)))
            hid = jnp.square(jnp.maximum(hid, 0.0))
            hid_scr[r, c:c + N_CHUNK] = hid.astype(_BF16)
    for r, x1 in zip(subs, x1s):
        z_scr[r, :] = DEEPNORM_ALPHA * x1 + _dot(hid_scr[r, :], _weight(w2_ref))


def _final_norm(z_scr, g2, b2, o_ref):
    for r in _sub_tiles(z_scr):
        o_ref[r, :] = _layer_norm(z_scr[r, :], g2, b2)


def _skewed_steps(first, stage_a, stage_b):
    i = pl.program_id(0)
    last = pl.num_programs(0) - 1

    @pl.when(i == 0)
    def _():
        first()
        stage_a()

    @pl.when(jnp.logical_and(i > 0, i < last))
    def _():
        stage_b()
        stage_a()

    @pl.when(i == last)
    def _():
        stage_b()


def _cast_slabs(refs):
    n = len(refs) // 2
    for src, dst in zip(refs[:n], refs[n:]):
        dst[...] = pltpu.bitcast(src[...].astype(_BF16), jnp.uint32)


def _load_weights_bf16(srcs, dsts, stage, sems):
    chunks = []
    for src, dst in zip(srcs, dsts):
        for r0 in range(0, dst.shape[0], STAGE_ROWS):
            for c0 in range(0, dst.shape[1], STAGE_COLS):
                chunks.append((src, dst, r0, c0))

    def copy(j):
        src, _, r0, c0 = chunks[j]
        slot = j % N_STAGE
        return pltpu.make_async_copy(
            src.at[0, pl.ds(r0, STAGE_ROWS), pl.ds(c0, STAGE_COLS)],
            stage.at[slot], sems.at[slot])

    for j in range(min(N_STAGE, len(chunks))):
        copy(j).start()
    for j, (_, dst, r0, c0) in enumerate(chunks):
        copy(j).wait()
        dst[r0:r0 + STAGE_ROWS, c0:c0 + STAGE_COLS] = stage[j % N_STAGE].astype(_BF16)
        if j + N_STAGE < len(chunks):
            copy(j + N_STAGE).start()


def _layer0_kernel(x_ref, bin_ref, vg_ref, vb_ref, ws_ref, bst_ref, bout_ref,
                   g1_ref, b1_ref, g2_ref, b2_ref,
                   win_hbm, wout_hbm, w1_hbm, w2_hbm,
                   cast0, cast1, cast2, cast3,
                   o_ref, cast0_out, cast1_out, cast2_out, cast3_out,
                   win_ref, wout_ref, w1_ref, w2_ref, stage, sems,
                   vm_scr, hid_scr, z_scr):
    vn_scr = hid_scr.at[:, 0:D_GATE]
    y_scr = hid_scr.at[:, D_GATE:2 * D_GATE]

    def first():
        _load_weights_bf16((win_hbm, wout_hbm, w1_hbm, w2_hbm),
                           (win_ref, wout_ref, w1_ref, w2_ref), stage, sems)

    def stage_b():
        _final_norm(z_scr, g2_ref[...], b2_ref[...], o_ref)

    def stage_a():
        _layer0_tile(x_ref, bin_ref, vg_ref, vb_ref, ws_ref, bst_ref, bout_ref, g1_ref, b1_ref,
                     (cast0, cast1, cast2, cast3, cast0_out, cast1_out, cast2_out, cast3_out),
                     win_ref, wout_ref, w1_ref, w2_ref, vm_scr, vn_scr, y_scr, hid_scr, z_scr)

    _skewed_steps(first, stage_a, stage_b)


def _layer0_tile(x_ref, bin_ref, vg_ref, vb_ref, ws_ref, bst_ref, bout_ref, g1_ref, b1_ref,
                 cast_refs, win_ref, wout_ref, w1_ref, w2_ref,
                 vm_scr, vn_scr, y_scr, hid_scr, z_scr):
    _cast_slabs(cast_refs)

    subs = _sub_tiles(x_ref)
    xs = [x_ref[r, :] for r in subs]
    xbs = [x.astype(_BF16) for x in xs]

    for r, xb in zip(subs, xbs):
        for c in range(0, D_GATE, N_CHUNK):
            cw = slice(D_GATE + c, D_GATE + c + N_CHUNK)
            vm_scr[r, c:c + N_CHUNK] = _gelu(_dot(xb, _weight(win_ref, cw)) + bin_ref[:, cw])
    for r in subs:
        vn_scr[r, :] = _layer_norm(vm_scr[r, :], vg_ref[...], vb_ref[...]).astype(_BF16)

    row_chunk = lax.broadcasted_iota(jnp.int32, (GMLP_BLOCK, GMLP_BLOCK), 0) // CHUNK
    col_chunk = lax.broadcasted_iota(jnp.int32, (GMLP_BLOCK, GMLP_BLOCK), 1) // CHUNK
    mask = row_chunk >= col_chunk
    for g in range(A_GROUPS):
        cols = slice(g * A_GROUP_DIM, (g + 1) * A_GROUP_DIM)
        wg = jnp.where(mask, ws_ref[g], 0.0).astype(_BF16)
        bias = bst_ref[:, g:g + 1]
        for blk in range(x_ref.shape[0] // GMLP_BLOCK):
            rows = slice(blk * GMLP_BLOCK, (blk + 1) * GMLP_BLOCK)
            vm_scr[rows, cols] = _dot(wg, vn_scr[rows, cols]) + bias

    for r, xb in zip(subs, xbs):
        for c in range(0, D_GATE, N_CHUNK):
            cw = slice(c, c + N_CHUNK)
            u = _gelu(_dot(xb, _weight(win_ref, cw)) + bin_ref[:, cw])
            y_scr[r, cw] = (u * vm_scr[r, cw]).astype(_BF16)

    x1s = []
    for r, x in zip(subs, xs):
        mix = _dot(y_scr[r, :], _weight(wout_ref)) + bout_ref[...]
        x1s.append(_layer_norm(DEEPNORM_ALPHA * x + mix, g1_ref[...], b1_ref[...]))
    _mlp(x1s, w1_ref, w2_ref, hid_scr, z_scr)


def _layer1_kernel(x_ref, win_ref, conv_ref, wout_ref, g1_ref, b1_ref, w1_ref,
                   w2_ref, g2_ref, b2_ref, o_ref, carry_scr, gate_scr, hid_scr, z_scr,
                   *, tiles_per_seq):
    def first():
        carry_scr[...] = jnp.zeros_like(carry_scr)

    def stage_b():
        _final_norm(z_scr, g2_ref[...], b2_ref[...], o_ref)

    def stage_a():
        _layer1_tile(x_ref, win_ref, conv_ref, wout_ref, g1_ref, b1_ref, w1_ref, w2_ref,
                     carry_scr, gate_scr, hid_scr, z_scr, tiles_per_seq)

    _skewed_steps(first, stage_a, stage_b)


def _layer1_tile(x_ref, win_ref, conv_ref, wout_ref, g1_ref, b1_ref, w1_ref, w2_ref,
                 carry_scr, gate_scr, hid_scr, z_scr, tiles_per_seq):
    subs = _sub_tiles(x_ref)
    xs = [x_ref[r, :] for r in subs]
    xbs = [x.astype(_BF16) for x in xs]
    seq_start = pl.program_id(0) % tiles_per_seq == 0
    prev = jnp.where(seq_start, 0.0, carry_scr[...])
    for r, xb in zip(subs, xbs):
        c_gate = _dot(xb, _weight(win_ref, slice(D_MODEL, 2 * D_MODEL)))
        hc = c_gate * _dot(xb, _weight(win_ref, slice(2 * D_MODEL, 3 * D_MODEL)))
        ext = jnp.concatenate([prev, hc], axis=0)
        prev = hc[-SUBLANES:, :]
        conv = conv_ref[CONV_WIDTH - 1:CONV_WIDTH, :] * hc
        for k in range(CONV_WIDTH - 1):
            shift = CONV_WIDTH - 1 - k
            conv = conv + conv_ref[k:k + 1, :] * pltpu.roll(ext, shift, axis=0)[SUBLANES:, :]
        b_gate = _dot(xb, _weight(win_ref, slice(0, D_MODEL)))
        gate_scr[r, :] = (b_gate * conv).astype(_BF16)
    carry_scr[...] = prev

    x1s = []
    for r, x in zip(subs, xs):
        mix = _dot(gate_scr[r, :], _weight(wout_ref))
        x1s.append(_layer_norm(DEEPNORM_ALPHA * x + mix, g1_ref[...], b1_ref[...]))
    _mlp(x1s, w1_ref, w2_ref, hid_scr, z_scr)


def _resident(shape):
    zeros = (0,) * len(shape)
    return pl.BlockSpec(shape, lambda i: zeros, pipeline_mode=pl.Buffered(1))


def _layer_slab(layer, tiles):
    return lambda i: (layer, jnp.minimum(i, tiles - 1), 0)


def _row(v):
    return v.reshape(1, -1)


def _run_layer(body, x2d, tm, consts, scratch_shapes, hbm=(), cast=()):
    m = x2d.shape[0]
    tiles = m // tm
    tile_in = pl.BlockSpec((tm, D_MODEL), lambda i: (jnp.minimum(i, tiles - 1), 0))
    tile_out = pl.BlockSpec((tm, D_MODEL), lambda i: (jnp.maximum(i - 1, 0), 0))
    slab_in = [pl.BlockSpec((None, w.shape[1] // tiles, w.shape[2]), _layer_slab(l, tiles))
               for w, l in cast]
    slab_out = [pl.BlockSpec((w.shape[1] // tiles // 2, w.shape[2]),
                             lambda i: (jnp.minimum(i, tiles - 1), 0)) for w, _ in cast]
    outs = pl.pallas_call(
        body,
        grid=(tiles + 1,),
        in_specs=[tile_in] + [_resident(c.shape) for c in consts]
        + [pl.BlockSpec(memory_space=pl.ANY)] * len(hbm) + slab_in,
        out_specs=[tile_out] + slab_out,
        out_shape=[jax.ShapeDtypeStruct((m, D_MODEL), _F32)]
        + [jax.ShapeDtypeStruct((w.shape[1] // 2, w.shape[2]), jnp.uint32) for w, _ in cast],
        scratch_shapes=scratch_shapes,
        compiler_params=pltpu.CompilerParams(
            dimension_semantics=("arbitrary",),
            vmem_limit_bytes=VMEM_LIMIT_BYTES),
    )(x2d, *consts, *hbm, *[w for w, _ in cast])
    return outs[0], outs[1:]


def kernel(x, ln_g, ln_b, a_w_in, a_b_in, a_v_g, a_v_b, a_w_s, a_b_s, a_w_out,
           a_b_out, b_w_in, b_conv, b_w_out, mlp_w1, mlp_w2):
    bsz, seq, d = x.shape
    assert d == D_MODEL and seq % TM_LAYER0 == 0 and seq % TM_LAYER1 == 0
    assert TM_LAYER0 % SUB == 0 and TM_LAYER1 % SUB == 0 and SUB % GMLP_BLOCK == 0
    x2d = x.reshape(bsz * seq, d)

    consts0 = [
        _row(a_b_in[0]), _row(a_v_g[0]), _row(a_v_b[0]), a_w_s[0], a_b_s[0].T,
        _row(a_b_out[0]), _row(ln_g[0, 0]), _row(ln_b[0, 0]),
        _row(ln_g[0, 1]), _row(ln_b[0, 1]),
    ]
    layer0_weights = (a_w_in, a_w_out, mlp_w1, mlp_w2)
    scratch0 = [
        pltpu.VMEM((D_MODEL, 2 * D_GATE), _BF16),
        pltpu.VMEM((D_GATE, D_MODEL), _BF16),
        pltpu.VMEM((D_MODEL, D_FF), _BF16),
        pltpu.VMEM((D_FF, D_MODEL), _BF16),
        pltpu.VMEM((N_STAGE, STAGE_ROWS, STAGE_COLS), _F32),
        pltpu.SemaphoreType.DMA((N_STAGE,)),
        pltpu.VMEM((TM_LAYER0, D_GATE), _F32),
        pltpu.VMEM((TM_LAYER0, D_FF), _BF16),
        pltpu.VMEM((TM_LAYER0, D_MODEL), _F32),
    ]
    layer1_weights = ((b_w_in, 0), (b_w_out, 0), (mlp_w1, 1), (mlp_w2, 1))
    x2d, (w_in1, w_out1, w1_1, w2_1) = _run_layer(
        _layer0_kernel, x2d, TM_LAYER0, consts0, scratch0,
        hbm=layer0_weights, cast=layer1_weights)

    consts1 = [
        w_in1, b_conv[0], w_out1,
        _row(ln_g[1, 0]), _row(ln_b[1, 0]),
        w1_1, w2_1,
        _row(ln_g[1, 1]), _row(ln_b[1, 1]),
    ]
    scratch1 = [
        pltpu.VMEM((SUBLANES, D_MODEL), _F32),
        pltpu.VMEM((TM_LAYER1, D_MODEL), _BF16),
        pltpu.VMEM((TM_LAYER1, D_FF), _BF16),
        pltpu.VMEM((TM_LAYER1, D_MODEL), _F32),
    ]
    body1 = functools.partial(_layer1_kernel, tiles_per_seq=seq // TM_LAYER1)
    x2d, _ = _run_layer(body1, x2d, TM_LAYER1, consts1, scratch1)
    return x2d.reshape(bsz, seq, d)
```

```python
import functools

import jax
import jax.numpy as jnp
from jax import lax
from jax.experimental import pallas as pl
from jax.experimental.pallas import tpu as pltpu

D_MODEL = 1024
D_GATE = 2 * D_MODEL
D_FF = 4 * D_MODEL
A_GROUPS = 8
A_GROUP_DIM = D_GATE // A_GROUPS
GMLP_BLOCK = 128
CHUNK = 64
CONV_WIDTH = 3
DEPTH = 2
LN_EPS = 1e-5
DEEPNORM_ALPHA = (2.0 * DEPTH) ** 0.25

SUBLANES = 8
TM_LAYER0 = 512
TM_LAYER1 = 512
SUB_ROWS = (384, 128)
N_CHUNK = 512
STAGE_ROWS, STAGE_COLS = 128, 1024
N_STAGE = 8
VMEM_LIMIT_BYTES = 62 * 1024 * 1024

_BF16 = jnp.bfloat16
_F32 = jnp.float32


def _layer_norm(z, g, b):
    mu = jnp.mean(z, axis=-1, keepdims=True)
    zc = z - mu
    var = jnp.mean(zc * zc, axis=-1, keepdims=True)
    return zc * lax.rsqrt(var + LN_EPS) * g + b


def _gelu(z):
    return 0.5 * z * (1.0 + lax.erf(z * (2.0 ** -0.5)))


def _dot(a, b):
    return jnp.dot(a, b, preferred_element_type=_F32)


def _sub_tiles(ref):
    assert ref.shape[0] == sum(SUB_ROWS)
    starts = [sum(SUB_ROWS[:k]) for k in range(len(SUB_ROWS))]
    return [slice(r0, r0 + n) for r0, n in zip(starts, SUB_ROWS)]


def _weight(ref, cols=slice(None)):
    w = ref[:, cols]
    return pltpu.bitcast(w, _BF16) if w.dtype == jnp.uint32 else w


def _mlp_and_norm(x1s, w1_ref, w2_ref, g2, b2, hid_scr, o_ref):
    subs = _sub_tiles(o_ref)
    for r, x1 in zip(subs, x1s):
        x1b = x1.astype(_BF16)
        for c in range(0, D_FF, N_CHUNK):
            hid = _dot(x1b, _weight(w1_ref, slice(c, c + N_CHUNK)))
            hid = jnp.square(jnp.maximum(hid, 0.0))
            hid_scr[r, c:c + N_CHUNK] = hid.astype(_BF16)
    for r, x1 in zip(subs, x1s):
        ff = _dot(hid_scr[r, :], _weight(w2_ref))
        o_ref[r, :] = _layer_norm(DEEPNORM_ALPHA * x1 + ff, g2, b2)


def _cast_slabs(refs):
    n = len(refs) // 2
    for src, dst in zip(refs[:n], refs[n:]):
        dst[...] = pltpu.bitcast(src[...].astype(_BF16), jnp.uint32)


def _load_weights_bf16(srcs, dsts, stage, sems):
    chunks = []
    for src, dst in zip(srcs, dsts):
        for r0 in range(0, dst.shape[0], STAGE_ROWS):
            for c0 in range(0, dst.shape[1], STAGE_COLS):
                chunks.append((src, dst, r0, c0))

    def copy(j):
        src, _, r0, c0 = chunks[j]
        slot = j % N_STAGE
        return pltpu.make_async_copy(
            src.at[0, pl.ds(r0, STAGE_ROWS), pl.ds(c0, STAGE_COLS)],
            stage.at[slot], sems.at[slot])

    for j in range(min(N_STAGE, len(chunks))):
        copy(j).start()
    for j, (_, dst, r0, c0) in enumerate(chunks):
        copy(j).wait()
        dst[r0:r0 + STAGE_ROWS, c0:c0 + STAGE_COLS] = stage[j % N_STAGE].astype(_BF16)
        if j + N_STAGE < len(chunks):
            copy(j + N_STAGE).start()


def _layer0_kernel(x_ref, bin_ref, vg_ref, vb_ref, ws_ref, bst_ref, bout_ref,
                   g1_ref, b1_ref, g2_ref, b2_ref,
                   win_hbm, wout_hbm, w1_hbm, w2_hbm,
                   cast0, cast1, cast2, cast3,
                   o_ref, cast0_out, cast1_out, cast2_out, cast3_out,
                   win_ref, wout_ref, w1_ref, w2_ref, stage, sems,
                   vm_scr, vn_scr, y_scr, hid_scr):
    @pl.when(pl.program_id(0) == 0)
    def _():
        _load_weights_bf16((win_hbm, wout_hbm, w1_hbm, w2_hbm),
                           (win_ref, wout_ref, w1_ref, w2_ref), stage, sems)

    _cast_slabs((cast0, cast1, cast2, cast3, cast0_out, cast1_out, cast2_out, cast3_out))

    subs = _sub_tiles(x_ref)
    xs = [x_ref[r, :] for r in subs]
    xbs = [x.astype(_BF16) for x in xs]

    for r, xb in zip(subs, xbs):
        for c in range(0, D_GATE, N_CHUNK):
            cw = slice(D_GATE + c, D_GATE + c + N_CHUNK)
            vm_scr[r, c:c + N_CHUNK] = _gelu(_dot(xb, _weight(win_ref, cw)) + bin_ref[:, cw])
    for r in subs:
        vn_scr[r, :] = _layer_norm(vm_scr[r, :], vg_ref[...], vb_ref[...]).astype(_BF16)

    row_chunk = lax.broadcasted_iota(jnp.int32, (GMLP_BLOCK, GMLP_BLOCK), 0) // CHUNK
    col_chunk = lax.broadcasted_iota(jnp.int32, (GMLP_BLOCK, GMLP_BLOCK), 1) // CHUNK
    mask = row_chunk >= col_chunk
    wgs = [jnp.where(mask, ws_ref[g], 0.0).astype(_BF16) for g in range(A_GROUPS)]
    for r in subs:
        for g in range(A_GROUPS):
            cols = slice(g * A_GROUP_DIM, (g + 1) * A_GROUP_DIM)
            bias = bst_ref[:, g:g + 1]
            for r0 in range(r.start, r.stop, GMLP_BLOCK):
                rows = slice(r0, r0 + GMLP_BLOCK)
                vm_scr[rows, cols] = _dot(wgs[g], vn_scr[rows, cols]) + bias

    for r, xb in zip(subs, xbs):
        for c in range(0, D_GATE, N_CHUNK):
            cw = slice(c, c + N_CHUNK)
            u = _gelu(_dot(xb, _weight(win_ref, cw)) + bin_ref[:, cw])
            y_scr[r, cw] = (u * vm_scr[r, cw]).astype(_BF16)

    x1s = []
    for r, x in zip(subs, xs):
        mix = _dot(y_scr[r, :], _weight(wout_ref)) + bout_ref[...]
        x1s.append(_layer_norm(DEEPNORM_ALPHA * x + mix, g1_ref[...], b1_ref[...]))
    _mlp_and_norm(x1s, w1_ref, w2_ref, g2_ref[...], b2_ref[...], hid_scr, o_ref)


def _layer1_kernel(x_ref, win_ref, conv_ref, wout_ref, g1_ref, b1_ref, w1_ref,
                   w2_ref, g2_ref, b2_ref, o_ref, carry_scr, gate_scr, hid_scr,
                   *, tiles_per_seq):
    @pl.when(pl.program_id(0) % tiles_per_seq == 0)
    def _():
        carry_scr[...] = jnp.zeros_like(carry_scr)

    subs = _sub_tiles(x_ref)
    xs = [x_ref[r, :] for r in subs]
    xbs = [x.astype(_BF16) for x in xs]
    prev = carry_scr[...]
    for r, xb in zip(subs, xbs):
        c_gate = _dot(xb, _weight(win_ref, slice(D_MODEL, 2 * D_MODEL)))
        hc = c_gate * _dot(xb, _weight(win_ref, slice(2 * D_MODEL, 3 * D_MODEL)))
        ext = jnp.concatenate([prev, hc], axis=0)
        prev = hc[-SUBLANES:, :]
        conv = conv_ref[CONV_WIDTH - 1:CONV_WIDTH, :] * hc
        for k in range(CONV_WIDTH - 1):
            shift = CONV_WIDTH - 1 - k
            conv = conv + conv_ref[k:k + 1, :] * pltpu.roll(ext, shift, axis=0)[SUBLANES:, :]
        b_gate = _dot(xb, _weight(win_ref, slice(0, D_MODEL)))
        gate_scr[r, :] = (b_gate * conv).astype(_BF16)
    carry_scr[...] = prev

    x1s = []
    for r, x in zip(subs, xs):
        mix = _dot(gate_scr[r, :], _weight(wout_ref))
        x1s.append(_layer_norm(DEEPNORM_ALPHA * x + mix, g1_ref[...], b1_ref[...]))
    _mlp_and_norm(x1s, w1_ref, w2_ref, g2_ref[...], b2_ref[...], hid_scr, o_ref)


def _resident(shape):
    zeros = (0,) * len(shape)
    return pl.BlockSpec(shape, lambda i: zeros, pipeline_mode=pl.Buffered(1))


def _layer_slab(layer):
    return lambda i: (layer, i, 0)


def _row(v):
    return v.reshape(1, -1)


def _run_layer(body, x2d, tm, consts, scratch_shapes, hbm=(), cast=()):
    m = x2d.shape[0]
    steps = m // tm
    tile = pl.BlockSpec((tm, D_MODEL), lambda i: (i, 0))
    slab_in = [pl.BlockSpec((None, w.shape[1] // steps, w.shape[2]), _layer_slab(l))
               for w, l in cast]
    slab_out = [pl.BlockSpec((w.shape[1] // steps // 2, w.shape[2]), lambda i: (i, 0))
                for w, _ in cast]
    outs = pl.pallas_call(
        body,
        grid=(steps,),
        in_specs=[tile] + [_resident(c.shape) for c in consts]
        + [pl.BlockSpec(memory_space=pl.ANY)] * len(hbm) + slab_in,
        out_specs=[tile] + slab_out,
        out_shape=[jax.ShapeDtypeStruct((m, D_MODEL), _F32)]
        + [jax.ShapeDtypeStruct((w.shape[1] // 2, w.shape[2]), jnp.uint32) for w, _ in cast],
        scratch_shapes=scratch_shapes,
        compiler_params=pltpu.CompilerParams(
            dimension_semantics=("arbitrary",),
            vmem_limit_bytes=VMEM_LIMIT_BYTES),
    )(x2d, *consts, *hbm, *[w for w, _ in cast])
    return outs[0], outs[1:]


def kernel(x, ln_g, ln_b, a_w_in, a_b_in, a_v_g, a_v_b, a_w_s, a_b_s, a_w_out,
           a_b_out, b_w_in, b_conv, b_w_out, mlp_w1, mlp_w2):
    bsz, seq, d = x.shape
    assert d == D_MODEL and seq % TM_LAYER0 == 0 and seq % TM_LAYER1 == 0
    assert TM_LAYER0 == TM_LAYER1 == sum(SUB_ROWS) and all(n % GMLP_BLOCK == 0 for n in SUB_ROWS)
    x2d = x.reshape(bsz * seq, d)

    consts0 = [
        _row(a_b_in[0]), _row(a_v_g[0]), _row(a_v_b[0]), a_w_s[0], a_b_s[0].T,
        _row(a_b_out[0]), _row(ln_g[0, 0]), _row(ln_b[0, 0]),
        _row(ln_g[0, 1]), _row(ln_b[0, 1]),
    ]
    layer0_weights = (a_w_in, a_w_out, mlp_w1, mlp_w2)
    scratch0 = [
        pltpu.VMEM((D_MODEL, 2 * D_GATE), _BF16),
        pltpu.VMEM((D_GATE, D_MODEL), _BF16),
        pltpu.VMEM((D_MODEL, D_FF), _BF16),
        pltpu.VMEM((D_FF, D_MODEL), _BF16),
        pltpu.VMEM((N_STAGE, STAGE_ROWS, STAGE_COLS), _F32),
        pltpu.SemaphoreType.DMA((N_STAGE,)),
        pltpu.VMEM((TM_LAYER0, D_GATE), _F32),
        pltpu.VMEM((TM_LAYER0, D_GATE), _BF16),
        pltpu.VMEM((TM_LAYER0, D_GATE), _BF16),
        pltpu.VMEM((TM_LAYER0, D_FF), _BF16),
    ]
    layer1_weights = ((b_w_in, 0), (b_w_out, 0), (mlp_w1, 1), (mlp_w2, 1))
    x2d, (w_in1, w_out1, w1_1, w2_1) = _run_layer(
        _layer0_kernel, x2d, TM_LAYER0, consts0, scratch0,
        hbm=layer0_weights, cast=layer1_weights)

    consts1 = [
        w_in1, b_conv[0], w_out1,
        _row(ln_g[1, 0]), _row(ln_b[1, 0]),
        w1_1, w2_1,
        _row(ln_g[1, 1]), _row(ln_b[1, 1]),
    ]
    scratch1 = [
        pltpu.VMEM((SUBLANES, D_MODEL), _F32),
        pltpu.VMEM((TM_LAYER1, D_MODEL), _BF16),
        pltpu.VMEM((TM_LAYER1, D_FF), _BF16),
    ]
    body1 = functools.partial(_layer1_kernel, tiles_per_seq=seq // TM_LAYER1)
    x2d, _ = _run_layer(body1, x2d, TM_LAYER1, consts1, scratch1)
    return x2d.reshape(bsz, seq, d)
```

```python
import functools

import jax
import jax.numpy as jnp
from jax import lax
from jax.experimental import pallas as pl
from jax.experimental.pallas import tpu as pltpu

D_MODEL = 1024
D_GATE = 2 * D_MODEL
D_FF = 4 * D_MODEL
A_GROUPS = 8
A_GROUP_DIM = D_GATE // A_GROUPS
GMLP_BLOCK = 128
CHUNK = 64
CONV_WIDTH = 3
DEPTH = 2
LN_EPS = 1e-5
DEEPNORM_ALPHA = (2.0 * DEPTH) ** 0.25

SUBLANES = 8
LANES = 128
TM_LAYER0 = 512
TM_LAYER1 = 512
SUB = 256
N_CHUNK = 512
STAGE_ROWS, STAGE_COLS = 128, 1024
N_STAGE = 8
VMEM_LIMIT_BYTES = 62 * 1024 * 1024

_BF16 = jnp.bfloat16
_F32 = jnp.float32


def _layer_norm(z, g, b, mu=None):
    if mu is None:
        mu = jnp.mean(z, axis=-1, keepdims=True)
    zc = z - mu
    var = jnp.mean(zc * zc, axis=-1, keepdims=True)
    return zc * lax.rsqrt(var + LN_EPS) * g + b


def _ln_params(lng_ref, lnb_ref, layer, sublayer):
    return (lng_ref[layer, sublayer:sublayer + 1, :], lnb_ref[layer, sublayer:sublayer + 1, :])


def _gelu(z):
    return 0.5 * z * (1.0 + lax.erf(z * (2.0 ** -0.5)))


def _dot(a, b):
    return jnp.dot(a, b, preferred_element_type=_F32)


def _sub_tiles(ref):
    return [slice(r0, r0 + SUB) for r0 in range(0, ref.shape[0], SUB)]


def _weight(ref, cols=slice(None)):
    w = ref[:, cols]
    return pltpu.bitcast(w, _BF16) if w.dtype == jnp.uint32 else w


def _mlp_and_norm(x1s, w1_ref, w2_ref, g2, b2, hid_scr, o_ref):
    subs = _sub_tiles(o_ref)
    for r, x1 in zip(subs, x1s):
        x1b = x1.astype(_BF16)
        for c in range(0, D_FF, N_CHUNK):
            hid = _dot(x1b, _weight(w1_ref, slice(c, c + N_CHUNK)))
            hid = jnp.square(jnp.maximum(hid, 0.0))
            hid_scr[r, c:c + N_CHUNK] = hid.astype(_BF16)
    for r, x1 in zip(subs, x1s):
        ff = _dot(hid_scr[r, :], _weight(w2_ref))
        o_ref[r, :] = _layer_norm(DEEPNORM_ALPHA * x1 + ff, g2, b2)


def _cast_slabs(refs):
    n = len(refs) // 2
    for src, dst in zip(refs[:n], refs[n:]):
        dst[...] = pltpu.bitcast(src[...].astype(_BF16), jnp.uint32)


def _load_weights_bf16(srcs, dsts, stage, sems):
    chunks = []
    for src, dst in zip(srcs, dsts):
        for r0 in range(0, dst.shape[0], STAGE_ROWS):
            for c0 in range(0, dst.shape[1], STAGE_COLS):
                chunks.append((src, dst, r0, c0))

    def copy(j):
        src, _, r0, c0 = chunks[j]
        slot = j % N_STAGE
        return pltpu.make_async_copy(
            src.at[0, pl.ds(r0, STAGE_ROWS), pl.ds(c0, STAGE_COLS)],
            stage.at[slot], sems.at[slot])

    for j in range(min(N_STAGE, len(chunks))):
        copy(j).start()
    for j, (_, dst, r0, c0) in enumerate(chunks):
        copy(j).wait()
        dst[r0:r0 + STAGE_ROWS, c0:c0 + STAGE_COLS] = stage[j % N_STAGE].astype(_BF16)
        if j + N_STAGE < len(chunks):
            copy(j + N_STAGE).start()


def _layer0_kernel(x_ref, bin_ref, vg_ref, vb_ref, ws_ref, bst_ref, bout_ref,
                   lng_ref, lnb_ref,
                   win_hbm, wout_hbm, w1_hbm, w2_hbm,
                   cast0, cast1, cast2, cast3,
                   o_ref, cast0_out, cast1_out, cast2_out, cast3_out,
                   win_ref, wout_ref, w1_ref, w2_ref, stage, sems,
                   vm_scr, vn_scr, y_scr, hid_scr):
    @pl.when(pl.program_id(0) == 0)
    def _():
        _load_weights_bf16((win_hbm, wout_hbm, w1_hbm, w2_hbm),
                           (win_ref, wout_ref, w1_ref, w2_ref), stage, sems)

    _cast_slabs((cast0, cast1, cast2, cast3, cast0_out, cast1_out, cast2_out, cast3_out))

    subs = _sub_tiles(x_ref)
    xs = [x_ref[r, :] for r in subs]
    xbs = [x.astype(_BF16) for x in xs]

    lane_sums = []
    for r, xb in zip(subs, xbs):
        acc = None
        for c in range(0, D_GATE, N_CHUNK):
            cw = slice(D_GATE + c, D_GATE + c + N_CHUNK)
            v = _gelu(_dot(xb, _weight(win_ref, cw)) + bin_ref[:, cw])
            vm_scr[r, c:c + N_CHUNK] = v
            for l in range(0, N_CHUNK, LANES):
                acc = v[:, l:l + LANES] if acc is None else acc + v[:, l:l + LANES]
        lane_sums.append(acc)
    for r, acc in zip(subs, lane_sums):
        mu = jnp.sum(acc, axis=-1, keepdims=True) * (1.0 / D_GATE)
        vn_scr[r, :] = _layer_norm(vm_scr[r, :], vg_ref[...], vb_ref[...], mu).astype(_BF16)

    row_chunk = lax.broadcasted_iota(jnp.int32, (GMLP_BLOCK, GMLP_BLOCK), 0) // CHUNK
    col_chunk = lax.broadcasted_iota(jnp.int32, (GMLP_BLOCK, GMLP_BLOCK), 1) // CHUNK
    mask = row_chunk >= col_chunk
    for g in range(A_GROUPS):
        cols = slice(g * A_GROUP_DIM, (g + 1) * A_GROUP_DIM)
        wg = jnp.where(mask, ws_ref[g], 0.0).astype(_BF16)
        bias = bst_ref[:, g:g + 1]
        for blk in range(x_ref.shape[0] // GMLP_BLOCK):
            rows = slice(blk * GMLP_BLOCK, (blk + 1) * GMLP_BLOCK)
            vm_scr[rows, cols] = _dot(wg, vn_scr[rows, cols]) + bias

    for r, xb in zip(subs, xbs):
        for c in range(0, D_GATE, N_CHUNK):
            cw = slice(c, c + N_CHUNK)
            u = _gelu(_dot(xb, _weight(win_ref, cw)) + bin_ref[:, cw])
            y_scr[r, cw] = (u * vm_scr[r, cw]).astype(_BF16)

    x1s = []
    for r, x in zip(subs, xs):
        mix = _dot(y_scr[r, :], _weight(wout_ref)) + bout_ref[...]
        x1s.append(_layer_norm(DEEPNORM_ALPHA * x + mix, *_ln_params(lng_ref, lnb_ref, 0, 0)))
    _mlp_and_norm(x1s, w1_ref, w2_ref, *_ln_params(lng_ref, lnb_ref, 0, 1), hid_scr, o_ref)


def _layer1_kernel(x_ref, win_ref, conv_ref, wout_ref, w1_ref, w2_ref, lng_ref, lnb_ref,
                   o_ref, carry_scr, gate_scr, hid_scr, *, tiles_per_seq):
    @pl.when(pl.program_id(0) % tiles_per_seq == 0)
    def _():
        carry_scr[...] = jnp.zeros_like(carry_scr)

    subs = _sub_tiles(x_ref)
    xs = [x_ref[r, :] for r in subs]
    xbs = [x.astype(_BF16) for x in xs]
    prev = carry_scr[...]
    for r, xb in zip(subs, xbs):
        c_gate = _dot(xb, _weight(win_ref, slice(D_MODEL, 2 * D_MODEL)))
        hc = c_gate * _dot(xb, _weight(win_ref, slice(2 * D_MODEL, 3 * D_MODEL)))
        ext = jnp.concatenate([prev, hc], axis=0)
        prev = hc[-SUBLANES:, :]
        conv = conv_ref[CONV_WIDTH - 1:CONV_WIDTH, :] * hc
        for k in range(CONV_WIDTH - 1):
            shift = CONV_WIDTH - 1 - k
            conv = conv + conv_ref[k:k + 1, :] * pltpu.roll(ext, shift, axis=0)[SUBLANES:, :]
        b_gate = _dot(xb, _weight(win_ref, slice(0, D_MODEL)))
        gate_scr[r, :] = (b_gate * conv).astype(_BF16)
    carry_scr[...] = prev

    x1s = []
    for r, x in zip(subs, xs):
        mix = _dot(gate_scr[r, :], _weight(wout_ref))
        x1s.append(_layer_norm(DEEPNORM_ALPHA * x + mix, *_ln_params(lng_ref, lnb_ref, 1, 0)))
    _mlp_and_norm(x1s, w1_ref, w2_ref, *_ln_params(lng_ref, lnb_ref, 1, 1), hid_scr, o_ref)


def _resident(shape):
    zeros = (0,) * len(shape)
    return pl.BlockSpec(shape, lambda i: zeros, pipeline_mode=pl.Buffered(1))


def _layer_slab(layer):
    return lambda i: (layer, i, 0)


def _row(v):
    return v.reshape(1, -1)


def _run_layer(body, x2d, tm, consts, scratch_shapes, hbm=(), cast=()):
    m = x2d.shape[0]
    steps = m // tm
    tile = pl.BlockSpec((tm, D_MODEL), lambda i: (i, 0))
    slab_in = [pl.BlockSpec((None, w.shape[1] // steps, w.shape[2]), _layer_slab(l))
               for w, l in cast]
    slab_out = [pl.BlockSpec((w.shape[1] // steps // 2, w.shape[2]), lambda i: (i, 0))
                for w, _ in cast]
    outs = pl.pallas_call(
        body,
        grid=(steps,),
        in_specs=[tile] + [_resident(c.shape) for c in consts]
        + [pl.BlockSpec(memory_space=pl.ANY)] * len(hbm) + slab_in,
        out_specs=[tile] + slab_out,
        out_shape=[jax.ShapeDtypeStruct((m, D_MODEL), _F32)]
        + [jax.ShapeDtypeStruct((w.shape[1] // 2, w.shape[2]), jnp.uint32) for w, _ in cast],
        scratch_shapes=scratch_shapes,
        compiler_params=pltpu.CompilerParams(
            dimension_semantics=("arbitrary",),
            vmem_limit_bytes=VMEM_LIMIT_BYTES),
    )(x2d, *consts, *hbm, *[w for w, _ in cast])
    return outs[0], outs[1:]


def kernel(x, ln_g, ln_b, a_w_in, a_b_in, a_v_g, a_v_b, a_w_s, a_b_s, a_w_out,
           a_b_out, b_w_in, b_conv, b_w_out, mlp_w1, mlp_w2):
    bsz, seq, d = x.shape
    assert d == D_MODEL and seq % TM_LAYER0 == 0 and seq % TM_LAYER1 == 0
    assert TM_LAYER0 % SUB == 0 and TM_LAYER1 % SUB == 0 and SUB % GMLP_BLOCK == 0
    x2d = x.reshape(bsz * seq, d)

    consts0 = [
        _row(a_b_in[0]), _row(a_v_g[0]), _row(a_v_b[0]), a_w_s[0], a_b_s[0].T,
        _row(a_b_out[0]), ln_g, ln_b,
    ]
    layer0_weights = (a_w_in, a_w_out, mlp_w1, mlp_w2)
    scratch0 = [
        pltpu.VMEM((D_MODEL, 2 * D_GATE), _BF16),
        pltpu.VMEM((D_GATE, D_MODEL), _BF16),
        pltpu.VMEM((D_MODEL, D_FF), _BF16),
        pltpu.VMEM((D_FF, D_MODEL), _BF16),
        pltpu.VMEM((N_STAGE, STAGE_ROWS, STAGE_COLS), _F32),
        pltpu.SemaphoreType.DMA((N_STAGE,)),
        pltpu.VMEM((TM_LAYER0, D_GATE), _F32),
        pltpu.VMEM((TM_LAYER0, D_GATE), _BF16),
        pltpu.VMEM((TM_LAYER0, D_GATE), _BF16),
        pltpu.VMEM((TM_LAYER0, D_FF), _BF16),
    ]
    layer1_weights = ((b_w_in, 0), (b_w_out, 0), (mlp_w1, 1), (mlp_w2, 1))
    x2d, (w_in1, w_out1, w1_1, w2_1) = _run_layer(
        _layer0_kernel, x2d, TM_LAYER0, consts0, scratch0,
        hbm=layer0_weights, cast=layer1_weights)

    consts1 = [w_in1, b_conv[0], w_out1, w1_1, w2_1, ln_g, ln_b]
    scratch1 = [
        pltpu.VMEM((SUBLANES, D_MODEL), _F32),
        pltpu.VMEM((TM_LAYER1, D_MODEL), _BF16),
        pltpu.VMEM((TM_LAYER1, D_FF), _BF16),
    ]
    body1 = functools.partial(_layer1_kernel, tiles_per_seq=seq // TM_LAYER1)
    x2d, _ = _run_layer(body1, x2d, TM_LAYER1, consts1, scratch1)
    return x2d.reshape(bsz, seq, d)
```

```python
import functools

import jax
import jax.numpy as jnp
from jax import lax
from jax.experimental import pallas as pl
from jax.experimental.pallas import tpu as pltpu

D_MODEL = 1024
D_GATE = 2 * D_MODEL
D_FF = 4 * D_MODEL
A_GROUPS = 8
A_GROUP_DIM = D_GATE // A_GROUPS
GMLP_BLOCK = 128
CHUNK = 64
CONV_WIDTH = 3
DEPTH = 2
LN_EPS = 1e-5
DEEPNORM_ALPHA = (2.0 * DEPTH) ** 0.25

SUBLANES = 8
TM_LAYER0 = 512
TM_LAYER1 = 512
SUB = 512
N_CHUNK = 512
STAGE_ROWS, STAGE_COLS = 128, 1024
N_STAGE = 8
VMEM_LIMIT_BYTES = 62 * 1024 * 1024

_BF16 = jnp.bfloat16
_F32 = jnp.float32


def _layer_norm(z, g, b):
    mu = jnp.mean(z, axis=-1, keepdims=True)
    zc = z - mu
    var = jnp.mean(zc * zc, axis=-1, keepdims=True)
    return zc * lax.rsqrt(var + LN_EPS) * g + b


def _ln_params(lng_ref, lnb_ref, layer, sublayer):
    return (lng_ref[layer, sublayer:sublayer + 1, :], lnb_ref[layer, sublayer:sublayer + 1, :])


def _gelu(z):
    return 0.5 * z * (1.0 + lax.erf(z * (2.0 ** -0.5)))


def _dot(a, b):
    return jnp.dot(a, b, preferred_element_type=_F32)


def _sub_tiles(ref):
    return [slice(r0, r0 + SUB) for r0 in range(0, ref.shape[0], SUB)]


def _weight(ref, cols=slice(None)):
    w = ref[:, cols]
    return pltpu.bitcast(w, _BF16) if w.dtype == jnp.uint32 else w


def _mlp_and_norm(x1s, w1_ref, w2_ref, g2, b2, hid_scr, o_ref):
    subs = _sub_tiles(o_ref)
    for r, x1 in zip(subs, x1s):
        x1b = x1.astype(_BF16)
        for c in range(0, D_FF, N_CHUNK):
            hid = _dot(x1b, _weight(w1_ref, slice(c, c + N_CHUNK)))
            hid = jnp.square(jnp.maximum(hid, 0.0))
            hid_scr[r, c:c + N_CHUNK] = hid.astype(_BF16)
    for r, x1 in zip(subs, x1s):
        ff = _dot(hid_scr[r, :], _weight(w2_ref))
        o_ref[r, :] = _layer_norm(DEEPNORM_ALPHA * x1 + ff, g2, b2)


def _cast_slabs(refs):
    n = len(refs) // 2
    for src, dst in zip(refs[:n], refs[n:]):
        dst[...] = pltpu.bitcast(src[...].astype(_BF16), jnp.uint32)


def _load_weights_bf16(srcs, dsts, stage, sems):
    chunks = []
    for src, dst in zip(srcs, dsts):
        for r0 in range(0, dst.shape[0], STAGE_ROWS):
            for c0 in range(0, dst.shape[1], STAGE_COLS):
                chunks.append((src, dst, r0, c0))

    def copy(j):
        src, _, r0, c0 = chunks[j]
        slot = j % N_STAGE
        return pltpu.make_async_copy(
            src.at[0, pl.ds(r0, STAGE_ROWS), pl.ds(c0, STAGE_COLS)],
            stage.at[slot], sems.at[slot])

    for j in range(min(N_STAGE, len(chunks))):
        copy(j).start()
    for j, (_, dst, r0, c0) in enumerate(chunks):
        copy(j).wait()
        dst[r0:r0 + STAGE_ROWS, c0:c0 + STAGE_COLS] = stage[j % N_STAGE].astype(_BF16)
        if j + N_STAGE < len(chunks):
            copy(j + N_STAGE).start()


def _layer0_kernel(x_ref, bin_ref, vg_ref, vb_ref, ws_ref, bst_ref, bout_ref,
                   lng_ref, lnb_ref,
                   win_hbm, wout_hbm, w1_hbm, w2_hbm,
                   cast0, cast1, cast2, cast3,
                   o_ref, cast0_out, cast1_out, cast2_out, cast3_out,
                   win_ref, wout_ref, w1_ref, w2_ref, stage, sems,
                   vm_scr, vn_scr, y_scr, hid_scr):
    @pl.when(pl.program_id(0) == 0)
    def _():
        _load_weights_bf16((win_hbm, wout_hbm, w1_hbm, w2_hbm),
                           (win_ref, wout_ref, w1_ref, w2_ref), stage, sems)

    _cast_slabs((cast0, cast1, cast2, cast3, cast0_out, cast1_out, cast2_out, cast3_out))

    subs = _sub_tiles(x_ref)
    xs = [x_ref[r, :] for r in subs]
    xbs = [x.astype(_BF16) for x in xs]

    for r, xb in zip(subs, xbs):
        for c in range(0, D_GATE, N_CHUNK):
            cw = slice(D_GATE + c, D_GATE + c + N_CHUNK)
            vm_scr[r, c:c + N_CHUNK] = _gelu(_dot(xb, _weight(win_ref, cw)) + bin_ref[0:1, cw])
    for r in subs:
        vn_scr[r, :] = _layer_norm(vm_scr[r, :], vg_ref[0:1, :], vb_ref[0:1, :]).astype(_BF16)

    row_chunk = lax.broadcasted_iota(jnp.int32, (GMLP_BLOCK, GMLP_BLOCK), 0) // CHUNK
    col_chunk = lax.broadcasted_iota(jnp.int32, (GMLP_BLOCK, GMLP_BLOCK), 1) // CHUNK
    mask = row_chunk >= col_chunk
    for g in range(A_GROUPS):
        cols = slice(g * A_GROUP_DIM, (g + 1) * A_GROUP_DIM)
        wg = jnp.where(mask, ws_ref[0, g], 0.0).astype(_BF16)
        bias = bst_ref[:, g:g + 1]
        for blk in range(x_ref.shape[0] // GMLP_BLOCK):
            rows = slice(blk * GMLP_BLOCK, (blk + 1) * GMLP_BLOCK)
            vm_scr[rows, cols] = _dot(wg, vn_scr[rows, cols]) + bias

    for r, xb in zip(subs, xbs):
        for c in range(0, D_GATE, N_CHUNK):
            cw = slice(c, c + N_CHUNK)
            u = _gelu(_dot(xb, _weight(win_ref, cw)) + bin_ref[0:1, cw])
            y_scr[r, cw] = (u * vm_scr[r, cw]).astype(_BF16)

    x1s = []
    for r, x in zip(subs, xs):
        mix = _dot(y_scr[r, :], _weight(wout_ref)) + bout_ref[0:1, :]
        x1s.append(_layer_norm(DEEPNORM_ALPHA * x + mix, *_ln_params(lng_ref, lnb_ref, 0, 0)))
    _mlp_and_norm(x1s, w1_ref, w2_ref, *_ln_params(lng_ref, lnb_ref, 0, 1), hid_scr, o_ref)


def _layer1_kernel(x_ref, win_ref, conv_ref, wout_ref, w1_ref, w2_ref, lng_ref, lnb_ref,
                   o_ref, carry_scr, gate_scr, hid_scr, *, tiles_per_seq):
    @pl.when(pl.program_id(0) % tiles_per_seq == 0)
    def _():
        carry_scr[...] = jnp.zeros_like(carry_scr)

    subs = _sub_tiles(x_ref)
    xs = [x_ref[r, :] for r in subs]
    xbs = [x.astype(_BF16) for x in xs]
    prev = carry_scr[...]
    for r, xb in zip(subs, xbs):
        c_gate = _dot(xb, _weight(win_ref, slice(D_MODEL, 2 * D_MODEL)))
        hc = c_gate * _dot(xb, _weight(win_ref, slice(2 * D_MODEL, 3 * D_MODEL)))
        ext = jnp.concatenate([prev, hc], axis=0)
        prev = hc[-SUBLANES:, :]
        conv = conv_ref[0, CONV_WIDTH - 1:CONV_WIDTH, :] * hc
        for k in range(CONV_WIDTH - 1):
            shift = CONV_WIDTH - 1 - k
            conv = conv + conv_ref[0, k:k + 1, :] * pltpu.roll(ext, shift, axis=0)[SUBLANES:, :]
        b_gate = _dot(xb, _weight(win_ref, slice(0, D_MODEL)))
        gate_scr[r, :] = (b_gate * conv).astype(_BF16)
    carry_scr[...] = prev

    x1s = []
    for r, x in zip(subs, xs):
        mix = _dot(gate_scr[r, :], _weight(wout_ref))
        x1s.append(_layer_norm(DEEPNORM_ALPHA * x + mix, *_ln_params(lng_ref, lnb_ref, 1, 0)))
    _mlp_and_norm(x1s, w1_ref, w2_ref, *_ln_params(lng_ref, lnb_ref, 1, 1), hid_scr, o_ref)


def _resident(shape):
    zeros = (0,) * len(shape)
    return pl.BlockSpec(shape, lambda i: zeros, pipeline_mode=pl.Buffered(1))


def _layer_slab(layer):
    return lambda i: (layer, i, 0)


def _run_layer(body, x2d, tm, consts, scratch_shapes, hbm=(), cast=()):
    m = x2d.shape[0]
    steps = m // tm
    tile = pl.BlockSpec((tm, D_MODEL), lambda i: (i, 0))
    slab_in = [pl.BlockSpec((None, w.shape[1] // steps, w.shape[2]), _layer_slab(l))
               for w, l in cast]
    slab_out = [pl.BlockSpec((w.shape[1] // steps // 2, w.shape[2]), lambda i: (i, 0))
                for w, _ in cast]
    outs = pl.pallas_call(
        body,
        grid=(steps,),
        in_specs=[tile] + [_resident(c.shape) for c in consts]
        + [pl.BlockSpec(memory_space=pl.ANY)] * len(hbm) + slab_in,
        out_specs=[tile] + slab_out,
        out_shape=[jax.ShapeDtypeStruct((m, D_MODEL), _F32)]
        + [jax.ShapeDtypeStruct((w.shape[1] // 2, w.shape[2]), jnp.uint32) for w, _ in cast],
        scratch_shapes=scratch_shapes,
        compiler_params=pltpu.CompilerParams(
            dimension_semantics=("arbitrary",),
            vmem_limit_bytes=VMEM_LIMIT_BYTES),
    )(x2d, *consts, *hbm, *[w for w, _ in cast])
    return outs[0], outs[1:]


def kernel(x, ln_g, ln_b, a_w_in, a_b_in, a_v_g, a_v_b, a_w_s, a_b_s, a_w_out,
           a_b_out, b_w_in, b_conv, b_w_out, mlp_w1, mlp_w2):
    bsz, seq, d = x.shape
    assert d == D_MODEL and seq % TM_LAYER0 == 0 and seq % TM_LAYER1 == 0
    assert TM_LAYER0 % SUB == 0 and TM_LAYER1 % SUB == 0 and SUB % GMLP_BLOCK == 0
    x2d = x.reshape(bsz * seq, d)

    consts0 = [a_b_in, a_v_g, a_v_b, a_w_s, a_b_s[0].T, a_b_out, ln_g, ln_b]
    layer0_weights = (a_w_in, a_w_out, mlp_w1, mlp_w2)
    scratch0 = [
        pltpu.VMEM((D_MODEL, 2 * D_GATE), _BF16),
        pltpu.VMEM((D_GATE, D_MODEL), _BF16),
        pltpu.VMEM((D_MODEL, D_FF), _BF16),
        pltpu.VMEM((D_FF, D_MODEL), _BF16),
        pltpu.VMEM((N_STAGE, STAGE_ROWS, STAGE_COLS), _F32),
        pltpu.SemaphoreType.DMA((N_STAGE,)),
        pltpu.VMEM((TM_LAYER0, D_GATE), _F32),
        pltpu.VMEM((TM_LAYER0, D_GATE), _BF16),
        pltpu.VMEM((TM_LAYER0, D_GATE), _BF16),
        pltpu.VMEM((TM_LAYER0, D_FF), _BF16),
    ]
    layer1_weights = ((b_w_in, 0), (b_w_out, 0), (mlp_w1, 1), (mlp_w2, 1))
    x2d, (w_in1, w_out1, w1_1, w2_1) = _run_layer(
        _layer0_kernel, x2d, TM_LAYER0, consts0, scratch0,
        hbm=layer0_weights, cast=layer1_weights)

    consts1 = [w_in1, b_conv, w_out1, w1_1, w2_1, ln_g, ln_b]
    scratch1 = [
        pltpu.VMEM((SUBLANES, D_MODEL), _F32),
        pltpu.VMEM((TM_LAYER1, D_MODEL), _BF16),
        pltpu.VMEM((TM_LAYER1, D_FF), _BF16),
    ]
    body1 = functools.partial(_layer1_kernel, tiles_per_seq=seq // TM_LAYER1)
    x2d, _ = _run_layer(body1, x2d, TM_LAYER1, consts1, scratch1)
    return x2d.reshape(bsz, seq, d)
```

```python
import functools

import jax
import jax.numpy as jnp
from jax import lax
from jax.experimental import pallas as pl
from jax.experimental.pallas import tpu as pltpu

D_MODEL = 1024
D_GATE = 2 * D_MODEL
D_FF = 4 * D_MODEL
A_GROUPS = 8
A_GROUP_DIM = D_GATE // A_GROUPS
GMLP_BLOCK = 128
CHUNK = 64
CONV_WIDTH = 3
DEPTH = 2
LN_EPS = 1e-5
DEEPNORM_ALPHA = (2.0 * DEPTH) ** 0.25

SUBLANES = 8
TM_LAYER0 = 512
TM_LAYER1 = 1024
N_SUB = 2
N_CHUNK = 512
STAGE_ROWS, STAGE_COLS = 128, 1024
N_STAGE = 8
VMEM_LIMIT_BYTES = 62 * 1024 * 1024

_BF16 = jnp.bfloat16
_F32 = jnp.float32


def _layer_norm(z, g, b):
    mu = jnp.mean(z, axis=-1, keepdims=True)
    zc = z - mu
    var = jnp.mean(zc * zc, axis=-1, keepdims=True)
    return zc * lax.rsqrt(var + LN_EPS) * g + b


def _ln_params(lng_ref, lnb_ref, layer, sublayer):
    return (lng_ref[layer, sublayer:sublayer + 1, :], lnb_ref[layer, sublayer:sublayer + 1, :])


def _gelu(z):
    return 0.5 * z * (1.0 + lax.erf(z * (2.0 ** -0.5)))


def _dot(a, b):
    return jnp.dot(a, b, preferred_element_type=_F32)


def _sub_tiles(ref):
    sub = ref.shape[0] // N_SUB
    return [slice(k * sub, (k + 1) * sub) for k in range(N_SUB)]


def _weight(ref, cols=slice(None)):
    w = ref[:, cols]
    return pltpu.bitcast(w, _BF16) if w.dtype == jnp.uint32 else w


def _mlp_and_norm(x1s, w1_ref, w2_ref, g2, b2, hid_scr, o_ref):
    subs = _sub_tiles(o_ref)
    for r, x1 in zip(subs, x1s):
        x1b = x1.astype(_BF16)
        for c in range(0, D_FF, N_CHUNK):
            hid = _dot(x1b, _weight(w1_ref, slice(c, c + N_CHUNK)))
            hid = jnp.square(jnp.maximum(hid, 0.0))
            hid_scr[r, c:c + N_CHUNK] = hid.astype(_BF16)
    for r, x1 in zip(subs, x1s):
        ff = _dot(hid_scr[r, :], _weight(w2_ref))
        o_ref[r, :] = _layer_norm(DEEPNORM_ALPHA * x1 + ff, g2, b2)


def _cast_slabs(refs):
    n = len(refs) // 2
    for src, dst in zip(refs[:n], refs[n:]):
        dst[...] = pltpu.bitcast(src[...].astype(_BF16), jnp.uint32)


def _load_weights_bf16(srcs, dsts, stage, sems):
    chunks = []
    for src, dst in zip(srcs, dsts):
        for r0 in range(0, dst.shape[0], STAGE_ROWS):
            for c0 in range(0, dst.shape[1], STAGE_COLS):
                chunks.append((src, dst, r0, c0))

    def copy(j):
        src, _, r0, c0 = chunks[j]
        slot = j % N_STAGE
        return pltpu.make_async_copy(
            src.at[0, pl.ds(r0, STAGE_ROWS), pl.ds(c0, STAGE_COLS)],
            stage.at[slot], sems.at[slot])

    for j in range(min(N_STAGE, len(chunks))):
        copy(j).start()
    for j, (_, dst, r0, c0) in enumerate(chunks):
        copy(j).wait()
        dst[r0:r0 + STAGE_ROWS, c0:c0 + STAGE_COLS] = stage[j % N_STAGE].astype(_BF16)
        if j + N_STAGE < len(chunks):
            copy(j + N_STAGE).start()


def _layer0_kernel(x_ref, bin_ref, vg_ref, vb_ref, ws_ref, bst_ref, bout_ref,
                   lng_ref, lnb_ref,
                   win_hbm, wout_hbm, w1_hbm, w2_hbm,
                   cast0, cast1, cast2, cast3,
                   o_ref, cast0_out, cast1_out, cast2_out, cast3_out,
                   win_ref, wout_ref, w1_ref, w2_ref, stage, sems,
                   vm_scr, vn_scr, y_scr, hid_scr):
    @pl.when(pl.program_id(0) == 0)
    def _():
        _load_weights_bf16((win_hbm, wout_hbm, w1_hbm, w2_hbm),
                           (win_ref, wout_ref, w1_ref, w2_ref), stage, sems)

    _cast_slabs((cast0, cast1, cast2, cast3, cast0_out, cast1_out, cast2_out, cast3_out))

    subs = _sub_tiles(x_ref)
    xs = [x_ref[r, :] for r in subs]
    xbs = [x.astype(_BF16) for x in xs]

    for r, xb in zip(subs, xbs):
        for c in range(0, D_GATE, N_CHUNK):
            cw = slice(D_GATE + c, D_GATE + c + N_CHUNK)
            vm_scr[r, c:c + N_CHUNK] = _gelu(_dot(xb, _weight(win_ref, cw)) + bin_ref[0:1, cw])
    for r in subs:
        vn_scr[r, :] = _layer_norm(vm_scr[r, :], vg_ref[0:1, :], vb_ref[0:1, :]).astype(_BF16)

    row_chunk = lax.broadcasted_iota(jnp.int32, (GMLP_BLOCK, GMLP_BLOCK), 0) // CHUNK
    col_chunk = lax.broadcasted_iota(jnp.int32, (GMLP_BLOCK, GMLP_BLOCK), 1) // CHUNK
    mask = row_chunk >= col_chunk
    for g in range(A_GROUPS):
        cols = slice(g * A_GROUP_DIM, (g + 1) * A_GROUP_DIM)
        wg = jnp.where(mask, ws_ref[0, g], 0.0).astype(_BF16)
        bias = bst_ref[:, g:g + 1]
        for blk in range(x_ref.shape[0] // GMLP_BLOCK):
            rows = slice(blk * GMLP_BLOCK, (blk + 1) * GMLP_BLOCK)
            vm_scr[rows, cols] = _dot(wg, vn_scr[rows, cols]) + bias

    for r, xb in zip(subs, xbs):
        for c in range(0, D_GATE, N_CHUNK):
            cw = slice(c, c + N_CHUNK)
            u = _gelu(_dot(xb, _weight(win_ref, cw)) + bin_ref[0:1, cw])
            y_scr[r, cw] = (u * vm_scr[r, cw]).astype(_BF16)

    x1s = []
    for r, x in zip(subs, xs):
        mix = _dot(y_scr[r, :], _weight(wout_ref)) + bout_ref[0:1, :]
        x1s.append(_layer_norm(DEEPNORM_ALPHA * x + mix, *_ln_params(lng_ref, lnb_ref, 0, 0)))
    _mlp_and_norm(x1s, w1_ref, w2_ref, *_ln_params(lng_ref, lnb_ref, 0, 1), hid_scr, o_ref)


def _layer1_kernel(x_ref, win_ref, conv_ref, wout_ref, w1_ref, w2_ref, lng_ref, lnb_ref,
                   o_ref, carry_scr, gate_scr, hid_scr, *, tiles_per_seq):
    @pl.when(pl.program_id(0) % tiles_per_seq == 0)
    def _():
        carry_scr[...] = jnp.zeros_like(carry_scr)

    subs = _sub_tiles(x_ref)
    xs = [x_ref[r, :] for r in subs]
    xbs = [x.astype(_BF16) for x in xs]
    prev = carry_scr[...]
    for r, xb in zip(subs, xbs):
        c_gate = _dot(xb, _weight(win_ref, slice(D_MODEL, 2 * D_MODEL)))
        hc = c_gate * _dot(xb, _weight(win_ref, slice(2 * D_MODEL, 3 * D_MODEL)))
        ext = jnp.concatenate([prev, hc], axis=0)
        prev = hc[-SUBLANES:, :]
        conv = conv_ref[0, CONV_WIDTH - 1:CONV_WIDTH, :] * hc
        for k in range(CONV_WIDTH - 1):
            shift = CONV_WIDTH - 1 - k
            conv = conv + conv_ref[0, k:k + 1, :] * pltpu.roll(ext, shift, axis=0)[SUBLANES:, :]
        b_gate = _dot(xb, _weight(win_ref, slice(0, D_MODEL)))
        gate_scr[r, :] = (b_gate * conv).astype(_BF16)
    carry_scr[...] = prev

    x1s = []
    for r, x in zip(subs, xs):
        mix = _dot(gate_scr[r, :], _weight(wout_ref))
        x1s.append(_layer_norm(DEEPNORM_ALPHA * x + mix, *_ln_params(lng_ref, lnb_ref, 1, 0)))
    _mlp_and_norm(x1s, w1_ref, w2_ref, *_ln_params(lng_ref, lnb_ref, 1, 1), hid_scr, o_ref)


def _resident(shape):
    zeros = (0,) * len(shape)
    return pl.BlockSpec(shape, lambda i: zeros, pipeline_mode=pl.Buffered(1))


def _layer_slab(layer):
    return lambda i: (layer, i, 0)


def _run_layer(body, x2d, tm, consts, scratch_shapes, hbm=(), cast=()):
    m = x2d.shape[0]
    steps = m // tm
    tile = pl.BlockSpec((tm, D_MODEL), lambda i: (i, 0))
    slab_in = [pl.BlockSpec((None, w.shape[1] // steps, w.shape[2]), _layer_slab(l))
               for w, l in cast]
    slab_out = [pl.BlockSpec((w.shape[1] // steps // 2, w.shape[2]), lambda i: (i, 0))
                for w, _ in cast]
    outs = pl.pallas_call(
        body,
        grid=(steps,),
        in_specs=[tile] + [_resident(c.shape) for c in consts]
        + [pl.BlockSpec(memory_space=pl.ANY)] * len(hbm) + slab_in,
        out_specs=[tile] + slab_out,
        out_shape=[jax.ShapeDtypeStruct((m, D_MODEL), _F32)]
        + [jax.ShapeDtypeStruct((w.shape[1] // 2, w.shape[2]), jnp.uint32) for w, _ in cast],
        scratch_shapes=scratch_shapes,
        compiler_params=pltpu.CompilerParams(
            dimension_semantics=("arbitrary",),
            vmem_limit_bytes=VMEM_LIMIT_BYTES),
    )(x2d, *consts, *hbm, *[w for w, _ in cast])
    return outs[0], outs[1:]


def kernel(x, ln_g, ln_b, a_w_in, a_b_in, a_v_g, a_v_b, a_w_s, a_b_s, a_w_out,
           a_b_out, b_w_in, b_conv, b_w_out, mlp_w1, mlp_w2):
    bsz, seq, d = x.shape
    assert d == D_MODEL and seq % TM_LAYER0 == 0 and seq % TM_LAYER1 == 0
    assert TM_LAYER0 % (N_SUB * GMLP_BLOCK) == 0 and TM_LAYER1 % (N_SUB * GMLP_BLOCK) == 0
    x2d = x.reshape(bsz * seq, d)

    consts0 = [a_b_in, a_v_g, a_v_b, a_w_s, a_b_s[0].T, a_b_out, ln_g, ln_b]
    layer0_weights = (a_w_in, a_w_out, mlp_w1, mlp_w2)
    scratch0 = [
        pltpu.VMEM((D_MODEL, 2 * D_GATE), _BF16),
        pltpu.VMEM((D_GATE, D_MODEL), _BF16),
        pltpu.VMEM((D_MODEL, D_FF), _BF16),
        pltpu.VMEM((D_FF, D_MODEL), _BF16),
        pltpu.VMEM((N_STAGE, STAGE_ROWS, STAGE_COLS), _F32),
        pltpu.SemaphoreType.DMA((N_STAGE,)),
        pltpu.VMEM((TM_LAYER0, D_GATE), _F32),
        pltpu.VMEM((TM_LAYER0, D_GATE), _BF16),
        pltpu.VMEM((TM_LAYER0, D_GATE), _BF16),
        pltpu.VMEM((TM_LAYER0, D_FF), _BF16),
    ]
    layer1_weights = ((b_w_in, 0), (b_w_out, 0), (mlp_w1, 1), (mlp_w2, 1))
    x2d, (w_in1, w_out1, w1_1, w2_1) = _run_layer(
        _layer0_kernel, x2d, TM_LAYER0, consts0, scratch0,
        hbm=layer0_weights, cast=layer1_weights)

    consts1 = [w_in1, b_conv, w_out1, w1_1, w2_1, ln_g, ln_b]
    scratch1 = [
        pltpu.VMEM((SUBLANES, D_MODEL), _F32),
        pltpu.VMEM((TM_LAYER1, D_MODEL), _BF16),
        pltpu.VMEM((TM_LAYER1, D_FF), _BF16),
    ]
    body1 = functools.partial(_layer1_kernel, tiles_per_seq=seq // TM_LAYER1)
    x2d, _ = _run_layer(body1, x2d, TM_LAYER1, consts1, scratch1)
    return x2d.reshape(bsz, seq, d)
```

```python
import functools

import jax
import jax.numpy as jnp
from jax import lax
from jax.experimental import pallas as pl
from jax.experimental.pallas import tpu as pltpu

D_MODEL = 1024
D_GATE = 2 * D_MODEL
D_FF = 4 * D_MODEL
A_GROUPS = 8
A_GROUP_DIM = D_GATE // A_GROUPS
GMLP_BLOCK = 128
CHUNK = 64
CONV_WIDTH = 3
DEPTH = 2
LN_EPS = 1e-5
DEEPNORM_ALPHA = (2.0 * DEPTH) ** 0.25

SUBLANES = 8
TM_LAYER0 = 512
TM_LAYER1 = 512
SUB = 256
N_CHUNK = 512
STAGE_ROWS, STAGE_COLS = 128, 1024
N_STAGE = 8
VMEM_LIMIT_BYTES = 62 * 1024 * 1024

_BF16 = jnp.bfloat16
_F32 = jnp.float32


def _layer_norm(z, g, b):
    mu = jnp.mean(z, axis=-1, keepdims=True)
    zc = z - mu
    var = jnp.mean(zc * zc, axis=-1, keepdims=True)
    return zc * lax.rsqrt(var + LN_EPS) * g + b


def _ln_params(lng_ref, lnb_ref, layer, sublayer):
    return (lng_ref[layer, sublayer:sublayer + 1, :], lnb_ref[layer, sublayer:sublayer + 1, :])


def _gelu(z):
    return 0.5 * z * (1.0 + lax.erf(z * (2.0 ** -0.5)))


def _dot(a, b):
    return jnp.dot(a, b, preferred_element_type=_F32)


def _sub_tiles(ref):
    return [slice(r0, r0 + SUB) for r0 in range(0, ref.shape[0], SUB)]


def _weight(ref, cols=slice(None)):
    w = ref[:, cols]
    return pltpu.bitcast(w, _BF16) if w.dtype == jnp.uint32 else w


def _mlp_and_norm(x1s, w1_ref, w2_ref, g2, b2, hid_scr, o_ref):
    subs = _sub_tiles(o_ref)
    for r, x1 in zip(subs, x1s):
        x1b = x1.astype(_BF16)
        for c in range(0, D_FF, N_CHUNK):
            hid = _dot(x1b, _weight(w1_ref, slice(c, c + N_CHUNK)))
            hid = jnp.square(jnp.maximum(hid, 0.0))
            hid_scr[r, c:c + N_CHUNK] = hid.astype(_BF16)
    for r, x1 in zip(subs, x1s):
        ff = _dot(hid_scr[r, :], _weight(w2_ref))
        o_ref[r, :] = _layer_norm(DEEPNORM_ALPHA * x1 + ff, g2, b2)


def _cast_slabs(refs):
    n = len(refs) // 2
    for src, dst in zip(refs[:n], refs[n:]):
        dst[...] = pltpu.bitcast(src[...].astype(_BF16), jnp.uint32)


def _load_weights_bf16(srcs, dsts, stage, sems):
    chunks = []
    for src, dst in zip(srcs, dsts):
        for r0 in range(0, dst.shape[0], STAGE_ROWS):
            for c0 in range(0, dst.shape[1], STAGE_COLS):
                chunks.append((src, dst, r0, c0))

    def copy(j):
        src, _, r0, c0 = chunks[j]
        slot = j % N_STAGE
        return pltpu.make_async_copy(
            src.at[0, pl.ds(r0, STAGE_ROWS), pl.ds(c0, STAGE_COLS)],
            stage.at[slot], sems.at[slot])

    for j in range(min(N_STAGE, len(chunks))):
        copy(j).start(priority=j % 2)
    for j, (_, dst, r0, c0) in enumerate(chunks):
        copy(j).wait()
        dst[r0:r0 + STAGE_ROWS, c0:c0 + STAGE_COLS] = stage[j % N_STAGE].astype(_BF16)
        if j + N_STAGE < len(chunks):
            copy(j + N_STAGE).start(priority=j % 2)


def _layer0_kernel(x_ref, bin_ref, vg_ref, vb_ref, ws_ref, bst_ref, bout_ref,
                   lng_ref, lnb_ref,
                   win_hbm, wout_hbm, w1_hbm, w2_hbm,
                   cast0, cast1, cast2, cast3,
                   o_ref, cast0_out, cast1_out, cast2_out, cast3_out,
                   win_ref, wout_ref, w1_ref, w2_ref, stage, sems,
                   vm_scr, vn_scr, y_scr, hid_scr):
    @pl.when(pl.program_id(0) == 0)
    def _():
        _load_weights_bf16((win_hbm, wout_hbm, w1_hbm, w2_hbm),
                           (win_ref, wout_ref, w1_ref, w2_ref), stage, sems)

    _cast_slabs((cast0, cast1, cast2, cast3, cast0_out, cast1_out, cast2_out, cast3_out))

    subs = _sub_tiles(x_ref)
    xs = [x_ref[r, :] for r in subs]
    xbs = [x.astype(_BF16) for x in xs]

    for r, xb in zip(subs, xbs):
        for c in range(0, D_GATE, N_CHUNK):
            cw = slice(D_GATE + c, D_GATE + c + N_CHUNK)
            vm_scr[r, c:c + N_CHUNK] = _gelu(_dot(xb, _weight(win_ref, cw)) + bin_ref[0:1, cw])
    for r in subs:
        vn_scr[r, :] = _layer_norm(vm_scr[r, :], vg_ref[0:1, :], vb_ref[0:1, :]).astype(_BF16)

    row_chunk = lax.broadcasted_iota(jnp.int32, (GMLP_BLOCK, GMLP_BLOCK), 0) // CHUNK
    col_chunk = lax.broadcasted_iota(jnp.int32, (GMLP_BLOCK, GMLP_BLOCK), 1) // CHUNK
    mask = row_chunk >= col_chunk
    for g in range(A_GROUPS):
        cols = slice(g * A_GROUP_DIM, (g + 1) * A_GROUP_DIM)
        wg = jnp.where(mask, ws_ref[0, g], 0.0).astype(_BF16)
        bias = bst_ref[:, g:g + 1]
        for blk in range(x_ref.shape[0] // GMLP_BLOCK):
            rows = slice(blk * GMLP_BLOCK, (blk + 1) * GMLP_BLOCK)
            vm_scr[rows, cols] = _dot(wg, vn_scr[rows, cols]) + bias

    for r, xb in zip(subs, xbs):
        for c in range(0, D_GATE, N_CHUNK):
            cw = slice(c, c + N_CHUNK)
            u = _gelu(_dot(xb, _weight(win_ref, cw)) + bin_ref[0:1, cw])
            y_scr[r, cw] = (u * vm_scr[r, cw]).astype(_BF16)

    x1s = []
    for r, x in zip(subs, xs):
        mix = _dot(y_scr[r, :], _weight(wout_ref)) + bout_ref[0:1, :]
        x1s.append(_layer_norm(DEEPNORM_ALPHA * x + mix, *_ln_params(lng_ref, lnb_ref, 0, 0)))
    _mlp_and_norm(x1s, w1_ref, w2_ref, *_ln_params(lng_ref, lnb_ref, 0, 1), hid_scr, o_ref)


def _layer1_kernel(x_ref, win_ref, conv_ref, wout_ref, w1_ref, w2_ref, lng_ref, lnb_ref,
                   o_ref, carry_scr, gate_scr, hid_scr, *, tiles_per_seq):
    @pl.when(pl.program_id(0) % tiles_per_seq == 0)
    def _():
        carry_scr[...] = jnp.zeros_like(carry_scr)

    subs = _sub_tiles(x_ref)
    xs = [x_ref[r, :] for r in subs]
    xbs = [x.astype(_BF16) for x in xs]
    prev = carry_scr[...]
    for r, xb in zip(subs, xbs):
        c_gate = _dot(xb, _weight(win_ref, slice(D_MODEL, 2 * D_MODEL)))
        hc = c_gate * _dot(xb, _weight(win_ref, slice(2 * D_MODEL, 3 * D_MODEL)))
        ext = jnp.concatenate([prev, hc], axis=0)
        prev = hc[-SUBLANES:, :]
        conv = conv_ref[0, CONV_WIDTH - 1:CONV_WIDTH, :] * hc
        for k in range(CONV_WIDTH - 1):
            shift = CONV_WIDTH - 1 - k
            conv = conv + conv_ref[0, k:k + 1, :] * pltpu.roll(ext, shift, axis=0)[SUBLANES:, :]
        b_gate = _dot(xb, _weight(win_ref, slice(0, D_MODEL)))
        gate_scr[r, :] = (b_gate * conv).astype(_BF16)
    carry_scr[...] = prev

    x1s = []
    for r, x in zip(subs, xs):
        mix = _dot(gate_scr[r, :], _weight(wout_ref))
        x1s.append(_layer_norm(DEEPNORM_ALPHA * x + mix, *_ln_params(lng_ref, lnb_ref, 1, 0)))
    _mlp_and_norm(x1s, w1_ref, w2_ref, *_ln_params(lng_ref, lnb_ref, 1, 1), hid_scr, o_ref)


def _resident(shape):
    zeros = (0,) * len(shape)
    return pl.BlockSpec(shape, lambda i: zeros, pipeline_mode=pl.Buffered(1))


def _layer_slab(layer):
    return lambda i: (layer, i, 0)


def _run_layer(body, x2d, tm, consts, scratch_shapes, hbm=(), cast=()):
    m = x2d.shape[0]
    steps = m // tm
    tile = pl.BlockSpec((tm, D_MODEL), lambda i: (i, 0))
    slab_in = [pl.BlockSpec((None, w.shape[1] // steps, w.shape[2]), _layer_slab(l))
               for w, l in cast]
    slab_out = [pl.BlockSpec((w.shape[1] // steps // 2, w.shape[2]), lambda i: (i, 0))
                for w, _ in cast]
    outs = pl.pallas_call(
        body,
        grid=(steps,),
        in_specs=[tile] + [_resident(c.shape) for c in consts]
        + [pl.BlockSpec(memory_space=pl.ANY)] * len(hbm) + slab_in,
        out_specs=[tile] + slab_out,
        out_shape=[jax.ShapeDtypeStruct((m, D_MODEL), _F32)]
        + [jax.ShapeDtypeStruct((w.shape[1] // 2, w.shape[2]), jnp.uint32) for w, _ in cast],
        scratch_shapes=scratch_shapes,
        compiler_params=pltpu.CompilerParams(
            dimension_semantics=("arbitrary",),
            vmem_limit_bytes=VMEM_LIMIT_BYTES),
    )(x2d, *consts, *hbm, *[w for w, _ in cast])
    return outs[0], outs[1:]


def kernel(x, ln_g, ln_b, a_w_in, a_b_in, a_v_g, a_v_b, a_w_s, a_b_s, a_w_out,
           a_b_out, b_w_in, b_conv, b_w_out, mlp_w1, mlp_w2):
    bsz, seq, d = x.shape
    assert d == D_MODEL and seq % TM_LAYER0 == 0 and seq % TM_LAYER1 == 0
    assert TM_LAYER0 % SUB == 0 and TM_LAYER1 % SUB == 0 and SUB % GMLP_BLOCK == 0
    x2d = x.reshape(bsz * seq, d)

    consts0 = [a_b_in, a_v_g, a_v_b, a_w_s, a_b_s[0].T, a_b_out, ln_g, ln_b]
    layer0_weights = (a_w_in, a_w_out, mlp_w1, mlp_w2)
    scratch0 = [
        pltpu.VMEM((D_MODEL, 2 * D_GATE), _BF16),
        pltpu.VMEM((D_GATE, D_MODEL), _BF16),
        pltpu.VMEM((D_MODEL, D_FF), _BF16),
        pltpu.VMEM((D_FF, D_MODEL), _BF16),
        pltpu.VMEM((N_STAGE, STAGE_ROWS, STAGE_COLS), _F32),
        pltpu.SemaphoreType.DMA((N_STAGE,)),
        pltpu.VMEM((TM_LAYER0, D_GATE), _F32),
        pltpu.VMEM((TM_LAYER0, D_GATE), _BF16),
        pltpu.VMEM((TM_LAYER0, D_GATE), _BF16),
        pltpu.VMEM((TM_LAYER0, D_FF), _BF16),
    ]
    layer1_weights = ((b_w_in, 0), (b_w_out, 0), (mlp_w1, 1), (mlp_w2, 1))
    x2d, (w_in1, w_out1, w1_1, w2_1) = _run_layer(
        _layer0_kernel, x2d, TM_LAYER0, consts0, scratch0,
        hbm=layer0_weights, cast=layer1_weights)

    consts1 = [w_in1, b_conv, w_out1, w1_1, w2_1, ln_g, ln_b]
    scratch1 = [
        pltpu.VMEM((SUBLANES, D_MODEL), _F32),
        pltpu.VMEM((TM_LAYER1, D_MODEL), _BF16),
        pltpu.VMEM((TM_LAYER1, D_FF), _BF16),
    ]
    body1 = functools.partial(_layer1_kernel, tiles_per_seq=seq // TM_LAYER1)
    x2d, _ = _run_layer(body1, x2d, TM_LAYER1, consts1, scratch1)
    return x2d.reshape(bsz, seq, d)
```

```python
import functools

import jax
import jax.numpy as jnp
from jax import lax
from jax.experimental import pallas as pl
from jax.experimental.pallas import tpu as pltpu

D_MODEL = 1024
D_GATE = 2 * D_MODEL
D_FF = 4 * D_MODEL
A_GROUPS = 8
A_GROUP_DIM = D_GATE // A_GROUPS
GMLP_BLOCK = 128
CHUNK = 64
CONV_WIDTH = 3
DEPTH = 2
LN_EPS = 1e-5
DEEPNORM_ALPHA = (2.0 * DEPTH) ** 0.25

SUBLANES = 8
TM_LAYER0 = 512
TM_LAYER1 = 512
SUB = 256
N_CHUNK = 512
STAGE_ROWS, STAGE_COLS = 128, 1024
N_STAGE = 8
VMEM_LIMIT_BYTES = 62 * 1024 * 1024

_BF16 = jnp.bfloat16
_F32 = jnp.float32


def _layer_norm(z, g, b):
    mu = jnp.mean(z, axis=-1, keepdims=True)
    zc = z - mu
    var = jnp.mean(zc * zc, axis=-1, keepdims=True)
    return zc * lax.rsqrt(var + LN_EPS) * g + b


def _ln_params(lng_ref, lnb_ref, layer, sublayer):
    return (lng_ref[layer, sublayer:sublayer + 1, :], lnb_ref[layer, sublayer:sublayer + 1, :])


def _gelu(z):
    return 0.5 * z * (1.0 + lax.erf(z * (2.0 ** -0.5)))


def _dot(a, b):
    return jnp.dot(a, b, preferred_element_type=_F32)


def _sub_tiles(ref):
    return [slice(r0, r0 + SUB) for r0 in range(0, ref.shape[0], SUB)]


def _weight(ref, cols=slice(None)):
    w = ref[:, cols]
    return pltpu.bitcast(w, _BF16) if w.dtype == jnp.uint32 else w


def _mlp_and_norm(x1s, w1_ref, w2_ref, g2, b2, hid_scr, o_ref):
    subs = _sub_tiles(o_ref)
    for r, x1 in zip(subs, x1s):
        x1b = x1.astype(_BF16)
        for c in range(0, D_FF, N_CHUNK):
            hid = _dot(x1b, _weight(w1_ref, slice(c, c + N_CHUNK)))
            hid = jnp.square(jnp.maximum(hid, 0.0))
            hid_scr[r, c:c + N_CHUNK] = hid.astype(_BF16)
        ff = _dot(hid_scr[r, :], _weight(w2_ref))
        o_ref[r, :] = _layer_norm(DEEPNORM_ALPHA * x1 + ff, g2, b2)


def _cast_slabs(refs):
    n = len(refs) // 2
    for src, dst in zip(refs[:n], refs[n:]):
        dst[...] = pltpu.bitcast(src[...].astype(_BF16), jnp.uint32)


def _load_weights_bf16(srcs, dsts, stage, sems):
    chunks = []
    for src, dst in zip(srcs, dsts):
        for r0 in range(0, dst.shape[0], STAGE_ROWS):
            for c0 in range(0, dst.shape[1], STAGE_COLS):
                chunks.append((src, dst, r0, c0))

    def copy(j):
        src, _, r0, c0 = chunks[j]
        slot = j % N_STAGE
        return pltpu.make_async_copy(
            src.at[0, pl.ds(r0, STAGE_ROWS), pl.ds(c0, STAGE_COLS)],
            stage.at[slot], sems.at[slot])

    for j in range(min(N_STAGE, len(chunks))):
        copy(j).start()
    for j, (_, dst, r0, c0) in enumerate(chunks):
        copy(j).wait()
        dst[r0:r0 + STAGE_ROWS, c0:c0 + STAGE_COLS] = stage[j % N_STAGE].astype(_BF16)
        if j + N_STAGE < len(chunks):
            copy(j + N_STAGE).start()


def _layer0_kernel(x_ref, bin_ref, vg_ref, vb_ref, ws_ref, bst_ref, bout_ref,
                   lng_ref, lnb_ref,
                   win_hbm, wout_hbm, w1_hbm, w2_hbm,
                   cast0, cast1, cast2, cast3,
                   o_ref, cast0_out, cast1_out, cast2_out, cast3_out,
                   win_ref, wout_ref, w1_ref, w2_ref, stage, sems,
                   vm_scr, vn_scr, y_scr, hid_scr):
    @pl.when(pl.program_id(0) == 0)
    def _():
        _load_weights_bf16((win_hbm, wout_hbm, w1_hbm, w2_hbm),
                           (win_ref, wout_ref, w1_ref, w2_ref), stage, sems)

    _cast_slabs((cast0, cast1, cast2, cast3, cast0_out, cast1_out, cast2_out, cast3_out))

    subs = _sub_tiles(x_ref)
    xs = [x_ref[r, :] for r in subs]
    xbs = [x.astype(_BF16) for x in xs]

    for r, xb in zip(subs, xbs):
        for c in range(0, D_GATE, N_CHUNK):
            cw = slice(D_GATE + c, D_GATE + c + N_CHUNK)
            vm_scr[r, c:c + N_CHUNK] = _gelu(_dot(xb, _weight(win_ref, cw)) + bin_ref[0:1, cw])
    for r in subs:
        vn_scr[r, :] = _layer_norm(vm_scr[r, :], vg_ref[0:1, :], vb_ref[0:1, :]).astype(_BF16)

    row_chunk = lax.broadcasted_iota(jnp.int32, (GMLP_BLOCK, GMLP_BLOCK), 0) // CHUNK
    col_chunk = lax.broadcasted_iota(jnp.int32, (GMLP_BLOCK, GMLP_BLOCK), 1) // CHUNK
    mask = row_chunk >= col_chunk
    for g in range(A_GROUPS):
        cols = slice(g * A_GROUP_DIM, (g + 1) * A_GROUP_DIM)
        wg = jnp.where(mask, ws_ref[0, g], 0.0).astype(_BF16)
        bias = bst_ref[:, g:g + 1]
        for blk in range(x_ref.shape[0] // GMLP_BLOCK):
            rows = slice(blk * GMLP_BLOCK, (blk + 1) * GMLP_BLOCK)
            vm_scr[rows, cols] = _dot(wg, vn_scr[rows, cols]) + bias

    for r, xb in zip(subs, xbs):
        for c in range(0, D_GATE, N_CHUNK):
            cw = slice(c, c + N_CHUNK)
            u = _gelu(_dot(xb, _weight(win_ref, cw)) + bin_ref[0:1, cw])
            y_scr[r, cw] = (u * vm_scr[r, cw]).astype(_BF16)

    x1s = []
    for r, x in zip(subs, xs):
        mix = _dot(y_scr[r, :], _weight(wout_ref)) + bout_ref[0:1, :]
        x1s.append(_layer_norm(DEEPNORM_ALPHA * x + mix, *_ln_params(lng_ref, lnb_ref, 0, 0)))
    _mlp_and_norm(x1s, w1_ref, w2_ref, *_ln_params(lng_ref, lnb_ref, 0, 1), hid_scr, o_ref)


def _layer1_kernel(x_ref, win_ref, conv_ref, wout_ref, w1_ref, w2_ref, lng_ref, lnb_ref,
                   o_ref, carry_scr, gate_scr, hid_scr, *, tiles_per_seq):
    @pl.when(pl.program_id(0) % tiles_per_seq == 0)
    def _():
        carry_scr[...] = jnp.zeros_like(carry_scr)

    subs = _sub_tiles(x_ref)
    xs = [x_ref[r, :] for r in subs]
    xbs = [x.astype(_BF16) for x in xs]
    prev = carry_scr[...]
    for r, xb in zip(subs, xbs):
        c_gate = _dot(xb, _weight(win_ref, slice(D_MODEL, 2 * D_MODEL)))
        hc = c_gate * _dot(xb, _weight(win_ref, slice(2 * D_MODEL, 3 * D_MODEL)))
        ext = jnp.concatenate([prev, hc], axis=0)
        prev = hc[-SUBLANES:, :]
        conv = conv_ref[0, CONV_WIDTH - 1:CONV_WIDTH, :] * hc
        for k in range(CONV_WIDTH - 1):
            shift = CONV_WIDTH - 1 - k
            conv = conv + conv_ref[0, k:k + 1, :] * pltpu.roll(ext, shift, axis=0)[SUBLANES:, :]
        b_gate = _dot(xb, _weight(win_ref, slice(0, D_MODEL)))
        gate_scr[r, :] = (b_gate * conv).astype(_BF16)
    carry_scr[...] = prev

    x1s = []
    for r, x in zip(subs, xs):
        mix = _dot(gate_scr[r, :], _weight(wout_ref))
        x1s.append(_layer_norm(DEEPNORM_ALPHA * x + mix, *_ln_params(lng_ref, lnb_ref, 1, 0)))
    _mlp_and_norm(x1s, w1_ref, w2_ref, *_ln_params(lng_ref, lnb_ref, 1, 1), hid_scr, o_ref)


def _resident(shape):
    zeros = (0,) * len(shape)
    return pl.BlockSpec(shape, lambda i: zeros, pipeline_mode=pl.Buffered(1))


def _layer_slab(layer):
    return lambda i: (layer, i, 0)


def _run_layer(body, x2d, tm, consts, scratch_shapes, hbm=(), cast=()):
    m = x2d.shape[0]
    steps = m // tm
    tile = pl.BlockSpec((tm, D_MODEL), lambda i: (i, 0))
    slab_in = [pl.BlockSpec((None, w.shape[1] // steps, w.shape[2]), _layer_slab(l))
               for w, l in cast]
    slab_out = [pl.BlockSpec((w.shape[1] // steps // 2, w.shape[2]), lambda i: (i, 0))
                for w, _ in cast]
    outs = pl.pallas_call(
        body,
        grid=(steps,),
        in_specs=[tile] + [_resident(c.shape) for c in consts]
        + [pl.BlockSpec(memory_space=pl.ANY)] * len(hbm) + slab_in,
        out_specs=[tile] + slab_out,
        out_shape=[jax.ShapeDtypeStruct((m, D_MODEL), _F32)]
        + [jax.ShapeDtypeStruct((w.shape[1] // 2, w.shape[2]), jnp.uint32) for w, _ in cast],
        scratch_shapes=scratch_shapes,
        compiler_params=pltpu.CompilerParams(
            dimension_semantics=("arbitrary",),
            vmem_limit_bytes=VMEM_LIMIT_BYTES),
    )(x2d, *consts, *hbm, *[w for w, _ in cast])
    return outs[0], outs[1:]


def kernel(x, ln_g, ln_b, a_w_in, a_b_in, a_v_g, a_v_b, a_w_s, a_b_s, a_w_out,
           a_b_out, b_w_in, b_conv, b_w_out, mlp_w1, mlp_w2):
    bsz, seq, d = x.shape
    assert d == D_MODEL and seq % TM_LAYER0 == 0 and seq % TM_LAYER1 == 0
    assert TM_LAYER0 % SUB == 0 and TM_LAYER1 % SUB == 0 and SUB % GMLP_BLOCK == 0
    x2d = x.reshape(bsz * seq, d)

    consts0 = [a_b_in, a_v_g, a_v_b, a_w_s, a_b_s[0].T, a_b_out, ln_g, ln_b]
    layer0_weights = (a_w_in, a_w_out, mlp_w1, mlp_w2)
    scratch0 = [
        pltpu.VMEM((D_MODEL, 2 * D_GATE), _BF16),
        pltpu.VMEM((D_GATE, D_MODEL), _BF16),
        pltpu.VMEM((D_MODEL, D_FF), _BF16),
        pltpu.VMEM((D_FF, D_MODEL), _BF16),
        pltpu.VMEM((N_STAGE, STAGE_ROWS, STAGE_COLS), _F32),
        pltpu.SemaphoreType.DMA((N_STAGE,)),
        pltpu.VMEM((TM_LAYER0, D_GATE), _F32),
        pltpu.VMEM((TM_LAYER0, D_GATE), _BF16),
        pltpu.VMEM((TM_LAYER0, D_GATE), _BF16),
        pltpu.VMEM((TM_LAYER0, D_FF), _BF16),
    ]
    layer1_weights = ((b_w_in, 0), (b_w_out, 0), (mlp_w1, 1), (mlp_w2, 1))
    x2d, (w_in1, w_out1, w1_1, w2_1) = _run_layer(
        _layer0_kernel, x2d, TM_LAYER0, consts0, scratch0,
        hbm=layer0_weights, cast=layer1_weights)

    consts1 = [w_in1, b_conv, w_out1, w1_1, w2_1, ln_g, ln_b]
    scratch1 = [
        pltpu.VMEM((SUBLANES, D_MODEL), _F32),
        pltpu.VMEM((TM_LAYER1, D_MODEL), _BF16),
        pltpu.VMEM((TM_LAYER1, D_FF), _BF16),
    ]
    body1 = functools.partial(_layer1_kernel, tiles_per_seq=seq // TM_LAYER1)
    x2d, _ = _run_layer(body1, x2d, TM_LAYER1, consts1, scratch1)
    return x2d.reshape(bsz, seq, d)
```

```python
import functools

import jax
import jax.numpy as jnp
from jax import lax
from jax.experimental import pallas as pl
from jax.experimental.pallas import tpu as pltpu

D_MODEL = 1024
D_GATE = 2 * D_MODEL
D_FF = 4 * D_MODEL
A_GROUPS = 8
A_GROUP_DIM = D_GATE // A_GROUPS
GMLP_BLOCK = 128
CHUNK = 64
CONV_WIDTH = 3
DEPTH = 2
LN_EPS = 1e-5
DEEPNORM_ALPHA = (2.0 * DEPTH) ** 0.25

SUBLANES = 8
TM_LAYER0 = 512
TM_LAYER1 = 512
SUB = 256
N_CHUNK = 512
STAGE_ROWS, STAGE_COLS = 128, 1024
N_STAGE = 8
V7X_VMEM_BYTES = 64 * 1024 * 1024
VMEM_LIMIT_BYTES = V7X_VMEM_BYTES - 2 * 1024 * 1024

_BF16 = jnp.bfloat16
_F32 = jnp.float32


def _layer_norm(z, g, b):
    mu = jnp.mean(z, axis=-1, keepdims=True)
    zc = z - mu
    var = jnp.mean(zc * zc, axis=-1, keepdims=True)
    return zc * lax.rsqrt(var + LN_EPS) * g + b


def _ln_params(lng_ref, lnb_ref, layer, sublayer):
    return (lng_ref[layer, sublayer:sublayer + 1, :], lnb_ref[layer, sublayer:sublayer + 1, :])


def _gelu(z):
    return 0.5 * z * (1.0 + lax.erf(z * (2.0 ** -0.5)))


def _dot(a, b):
    return jnp.dot(a, b, preferred_element_type=_F32)


def _sub_tiles(ref):
    return [slice(r0, r0 + SUB) for r0 in range(0, ref.shape[0], SUB)]


def _weight(ref, cols=slice(None)):
    w = ref[:, cols]
    return pltpu.bitcast(w, _BF16) if w.dtype == jnp.uint32 else w


def _mlp_and_norm(x1s, w1_ref, w2_ref, g2, b2, hid_scr, o_ref):
    subs = _sub_tiles(o_ref)
    for r, x1 in zip(subs, x1s):
        x1b = x1.astype(_BF16)
        for c in range(0, D_FF, N_CHUNK):
            hid = _dot(x1b, _weight(w1_ref, slice(c, c + N_CHUNK)))
            hid = jnp.square(jnp.maximum(hid, 0.0))
            hid_scr[r, c:c + N_CHUNK] = hid.astype(_BF16)
    for r, x1 in zip(subs, x1s):
        ff = _dot(hid_scr[r, :], _weight(w2_ref))
        o_ref[r, :] = _layer_norm(DEEPNORM_ALPHA * x1 + ff, g2, b2)


def _cast_slabs(refs):
    n = len(refs) // 2
    for src, dst in zip(refs[:n], refs[n:]):
        dst[...] = pltpu.bitcast(src[...].astype(_BF16), jnp.uint32)


def _load_weights_bf16(srcs, dsts, stage, sems):
    chunks = []
    for src, dst in zip(srcs, dsts):
        for r0 in range(0, dst.shape[0], STAGE_ROWS):
            for c0 in range(0, dst.shape[1], STAGE_COLS):
                chunks.append((src, dst, r0, c0))

    def copy(j):
        src, _, r0, c0 = chunks[j]
        slot = j % N_STAGE
        return pltpu.make_async_copy(
            src.at[0, pl.ds(r0, STAGE_ROWS), pl.ds(c0, STAGE_COLS)],
            stage.at[slot], sems.at[slot])

    for j in range(min(N_STAGE, len(chunks))):
        copy(j).start()
    for j, (_, dst, r0, c0) in enumerate(chunks):
        copy(j).wait()
        dst[r0:r0 + STAGE_ROWS, c0:c0 + STAGE_COLS] = stage[j % N_STAGE].astype(_BF16)
        if j + N_STAGE < len(chunks):
            copy(j + N_STAGE).start()


def _layer0_kernel(x_ref, bin_ref, vg_ref, vb_ref, ws_ref, bst_ref, bout_ref,
                   lng_ref, lnb_ref,
                   win_hbm, wout_hbm, w1_hbm, w2_hbm,
                   cast0, cast1, cast2, cast3,
                   o_ref, cast0_out, cast1_out, cast2_out, cast3_out,
                   win_ref, wout_ref, w1_ref, w2_ref, stage, sems,
                   vm_scr, vn_scr, y_scr, hid_scr):
    @pl.when(pl.program_id(0) == 0)
    def _():
        _load_weights_bf16((win_hbm, wout_hbm, w1_hbm, w2_hbm),
                           (win_ref, wout_ref, w1_ref, w2_ref), stage, sems)

    _cast_slabs((cast0, cast1, cast2, cast3, cast0_out, cast1_out, cast2_out, cast3_out))

    subs = _sub_tiles(x_ref)
    xs = [x_ref[r, :] for r in subs]
    xbs = [x.astype(_BF16) for x in xs]

    for r, xb in zip(subs, xbs):
        for c in range(0, D_GATE, N_CHUNK):
            cw = slice(D_GATE + c, D_GATE + c + N_CHUNK)
            vm_scr[r, c:c + N_CHUNK] = _gelu(_dot(xb, _weight(win_ref, cw)) + bin_ref[0:1, cw])
    for r in subs:
        vn_scr[r, :] = _layer_norm(vm_scr[r, :], vg_ref[0:1, :], vb_ref[0:1, :]).astype(_BF16)

    row_chunk = lax.broadcasted_iota(jnp.int32, (GMLP_BLOCK, GMLP_BLOCK), 0) // CHUNK
    col_chunk = lax.broadcasted_iota(jnp.int32, (GMLP_BLOCK, GMLP_BLOCK), 1) // CHUNK
    mask = row_chunk >= col_chunk
    for g in range(A_GROUPS):
        cols = slice(g * A_GROUP_DIM, (g + 1) * A_GROUP_DIM)
        wg = jnp.where(mask, ws_ref[0, g], 0.0).astype(_BF16)
        bias = bst_ref[:, g:g + 1]
        for blk in range(x_ref.shape[0] // GMLP_BLOCK):
            rows = slice(blk * GMLP_BLOCK, (blk + 1) * GMLP_BLOCK)
            vm_scr[rows, cols] = _dot(wg, vn_scr[rows, cols]) + bias

    for r, xb in zip(subs, xbs):
        for c in range(0, D_GATE, N_CHUNK):
            cw = slice(c, c + N_CHUNK)
            u = _gelu(_dot(xb, _weight(win_ref, cw)) + bin_ref[0:1, cw])
            y_scr[r, cw] = (u * vm_scr[r, cw]).astype(_BF16)

    x1s = []
    for r, x in zip(subs, xs):
        mix = _dot(y_scr[r, :], _weight(wout_ref)) + bout_ref[0:1, :]
        x1s.append(_layer_norm(DEEPNORM_ALPHA * x + mix, *_ln_params(lng_ref, lnb_ref, 0, 0)))
    _mlp_and_norm(x1s, w1_ref, w2_ref, *_ln_params(lng_ref, lnb_ref, 0, 1), hid_scr, o_ref)


def _layer1_kernel(x_ref, win_ref, conv_ref, wout_ref, w1_ref, w2_ref, lng_ref, lnb_ref,
                   o_ref, carry_scr, gate_scr, hid_scr, *, tiles_per_seq):
    @pl.when(pl.program_id(0) % tiles_per_seq == 0)
    def _():
        carry_scr[...] = jnp.zeros_like(carry_scr)

    subs = _sub_tiles(x_ref)
    xs = [x_ref[r, :] for r in subs]
    xbs = [x.astype(_BF16) for x in xs]
    prev = carry_scr[...]
    for r, xb in zip(subs, xbs):
        c_gate = _dot(xb, _weight(win_ref, slice(D_MODEL, 2 * D_MODEL)))
        hc = c_gate * _dot(xb, _weight(win_ref, slice(2 * D_MODEL, 3 * D_MODEL)))
        ext = jnp.concatenate([prev, hc], axis=0)
        prev = hc[-SUBLANES:, :]
        conv = conv_ref[0, CONV_WIDTH - 1:CONV_WIDTH, :] * hc
        for k in range(CONV_WIDTH - 1):
            shift = CONV_WIDTH - 1 - k
            conv = conv + conv_ref[0, k:k + 1, :] * pltpu.roll(ext, shift, axis=0)[SUBLANES:, :]
        b_gate = _dot(xb, _weight(win_ref, slice(0, D_MODEL)))
        gate_scr[r, :] = (b_gate * conv).astype(_BF16)
    carry_scr[...] = prev

    x1s = []
    for r, x in zip(subs, xs):
        mix = _dot(gate_scr[r, :], _weight(wout_ref))
        x1s.append(_layer_norm(DEEPNORM_ALPHA * x + mix, *_ln_params(lng_ref, lnb_ref, 1, 0)))
    _mlp_and_norm(x1s, w1_ref, w2_ref, *_ln_params(lng_ref, lnb_ref, 1, 1), hid_scr, o_ref)


def _resident(shape):
    zeros = (0,) * len(shape)
    return pl.BlockSpec(shape, lambda i: zeros, pipeline_mode=pl.Buffered(1))


def _layer_slab(layer):
    return lambda i: (layer, i, 0)


def _run_layer(body, x2d, tm, consts, scratch_shapes, hbm=(), cast=()):
    m = x2d.shape[0]
    steps = m // tm
    tile = pl.BlockSpec((tm, D_MODEL), lambda i: (i, 0))
    slab_in = [pl.BlockSpec((None, w.shape[1] // steps, w.shape[2]), _layer_slab(l))
               for w, l in cast]
    slab_out = [pl.BlockSpec((w.shape[1] // steps // 2, w.shape[2]), lambda i: (i, 0))
                for w, _ in cast]
    outs = pl.pallas_call(
        body,
        grid=(steps,),
        in_specs=[tile] + [_resident(c.shape) for c in consts]
        + [pl.BlockSpec(memory_space=pl.ANY)] * len(hbm) + slab_in,
        out_specs=[tile] + slab_out,
        out_shape=[jax.ShapeDtypeStruct((m, D_MODEL), _F32)]
        + [jax.ShapeDtypeStruct((w.shape[1] // 2, w.shape[2]), jnp.uint32) for w, _ in cast],
        scratch_shapes=scratch_shapes,
        compiler_params=pltpu.CompilerParams(
            dimension_semantics=("arbitrary",),
            vmem_limit_bytes=VMEM_LIMIT_BYTES),
    )(x2d, *consts, *hbm, *[w for w, _ in cast])
    return outs[0], outs[1:]


def kernel(x, ln_g, ln_b, a_w_in, a_b_in, a_v_g, a_v_b, a_w_s, a_b_s, a_w_out,
           a_b_out, b_w_in, b_conv, b_w_out, mlp_w1, mlp_w2):
    bsz, seq, d = x.shape
    assert d == D_MODEL and seq % TM_LAYER0 == 0 and seq % TM_LAYER1 == 0
    assert TM_LAYER0 % SUB == 0 and TM_LAYER1 % SUB == 0 and SUB % GMLP_BLOCK == 0
    x2d = x.reshape(bsz * seq, d)

    consts0 = [a_b_in, a_v_g, a_v_b, a_w_s, a_b_s[0].T, a_b_out, ln_g, ln_b]
    layer0_weights = (a_w_in, a_w_out, mlp_w1, mlp_w2)
    scratch0 = [
        pltpu.VMEM((D_MODEL, 2 * D_GATE), _BF16),
        pltpu.VMEM((D_GATE, D_MODEL), _BF16),
        pltpu.VMEM((D_MODEL, D_FF), _BF16),
        pltpu.VMEM((D_FF, D_MODEL), _BF16),
        pltpu.VMEM((N_STAGE, STAGE_ROWS, STAGE_COLS), _F32),
        pltpu.SemaphoreType.DMA((N_STAGE,)),
        pltpu.VMEM((TM_LAYER0, D_GATE), _F32),
        pltpu.VMEM((TM_LAYER0, D_GATE), _BF16),
        pltpu.VMEM((TM_LAYER0, D_GATE), _BF16),
        pltpu.VMEM((TM_LAYER0, D_FF), _BF16),
    ]
    layer1_weights = ((b_w_in, 0), (b_w_out, 0), (mlp_w1, 1), (mlp_w2, 1))
    x2d, (w_in1, w_out1, w1_1, w2_1) = _run_layer(
        _layer0_kernel, x2d, TM_LAYER0, consts0, scratch0,
        hbm=layer0_weights, cast=layer1_weights)

    consts1 = [w_in1, b_conv, w_out1, w1_1, w2_1, ln_g, ln_b]
    scratch1 = [
        pltpu.VMEM((SUBLANES, D_MODEL), _F32),
        pltpu.VMEM((TM_LAYER1, D_MODEL), _BF16),
        pltpu.VMEM((TM_LAYER1, D_FF), _BF16),
    ]
    body1 = functools.partial(_layer1_kernel, tiles_per_seq=seq // TM_LAYER1)
    x2d, _ = _run_layer(body1, x2d, TM_LAYER1, consts1, scratch1)
    return x2d.reshape(bsz, seq, d)
```
